```python
import math
import jax
import jax.numpy as jnp
from jax import lax
import numpy as np


D_MODEL = 2048
BATCH = 8
SEQ = 2048
DEPTH = 2
DEC_BATCH = 128
DEC_SEQ = 1
PAST_LEN = 2048
PAGE_SIZE = 128

HEAD_DIM = 128
N_HEADS = D_MODEL // HEAD_DIM
H_A = N_HEADS // 2
H_KV_A = H_A // 2
H_B = N_HEADS - H_A
G_B = 2
H_C = N_HEADS
MOBA_BLOCK = 256
MOBA_TOPK = 3
CMP_LEN = 32
CMP_STRIDE = 16
CMP_HID = HEAD_DIM
SLC_BLOCK = 64
SLC_TOPK = 16
WIN_B = 512
C_PATTERNS = ((128, 1), (512, 4), (2048, 16))
WIN_C = 2048
N_BUCKETS = 32
T5_MAX_DIST = 128
D_FF = ((8 * D_MODEL // 3 + 127) // 128) * 128
BAND_BLOCK = 128
Q_CHUNK = 16
EPS = 1e-6
TINY = 1e-30
SCALE = HEAD_DIM ** -0.5
AB_SPLITS = (H_A * HEAD_DIM, H_KV_A * HEAD_DIM, H_KV_A * HEAD_DIM, H_B * HEAD_DIM) + (G_B * HEAD_DIM,) * 6 + (3 * H_B,)
AB_HEADS = (H_A, H_KV_A, H_KV_A, H_B) + (G_B,) * 6 + (H_B,)
C_SPLITS = (H_C * HEAD_DIM,) * 3
C_HEADS = (H_C,) * 3

kernel_name = 'moba_nsa_dilated_hybrid_decode_step'


def rms_norm(x, g):
    xf = x.astype(jnp.float32)
    y = xf * lax.rsqrt(jnp.mean(xf * xf, axis=-1, keepdims=True) + EPS)
    return (y * g.astype(jnp.float32)).astype(x.dtype)


def swiglu(x, w_in, w_out):
    gate, up = jnp.split(x @ w_in, 2, axis=-1)
    return (jax.nn.silu(gate) * up) @ w_out


def macaron_half(x, g_pre, g_post, w_in, w_out):
    return x + 0.5 * rms_norm(swiglu(rms_norm(x, g_pre), w_in, w_out), g_post)


def t5_bucket(dist):
    exact = N_BUCKETS // 2
    d = jnp.maximum(dist, 1).astype(jnp.float32)
    far = exact + (jnp.log(d / exact) / math.log(T5_MAX_DIST / exact) * (N_BUCKETS - exact)).astype(jnp.int32)
    return jnp.where(dist < exact, dist, jnp.minimum(far, N_BUCKETS - 1))


def masked_softmax(logits, mask):
    s = jnp.where(mask, logits.astype(jnp.float32), -jnp.inf)
    m = jnp.max(s, axis=-1, keepdims=True)
    m = jnp.where(jnp.isfinite(m), m, 0.0)
    e = jnp.exp(s - m)
    l = jnp.sum(e, axis=-1, keepdims=True)
    return e / jnp.maximum(l, TINY), m[..., 0], l[..., 0]


def split_heads(z, sizes, heads):
    parts = jnp.split(z, np.cumsum(sizes)[:-1].tolist(), axis=-1)
    return [p.reshape(p.shape[:2] + (h, -1)) for p, h in zip(parts, heads)]


def map_query_chunks(fn, q, q_pos):
    B, T = q.shape[:2]
    c = min(Q_CHUNK, T)
    n = -(-T // c)
    pad = n * c - T
    qp = jnp.pad(q, ((0, 0), (0, pad)) + ((0, 0),) * (q.ndim - 2))
    pp = jnp.pad(q_pos, (0, pad), mode='edge').reshape(n, c)
    qs = jnp.moveaxis(qp.reshape((B, n, c) + q.shape[2:]), 1, 0)
    out = lax.map(lambda a: fn(a[0], a[1]), (qs, pp))
    return jax.tree_util.tree_map(lambda o: jnp.moveaxis(o, 0, 1).reshape((B, n * c) + o.shape[3:])[:, :T], out)


def band_attention(q, k, v, max_back, dist_scale, bias_hT):
    B, L, H, hd = q.shape
    G = k.shape[2]
    rep = H // G
    nb = -(-L // BAND_BLOCK)
    lp = nb * BAND_BLOCK
    n_prev = -(-max_back // BAND_BLOCK)
    kw = (n_prev + 1) * BAND_BLOCK
    kpad = ((0, 0), (n_prev * BAND_BLOCK, lp - L), (0, 0), (0, 0))
    kb = jnp.pad(k, kpad).reshape(B, nb + n_prev, BAND_BLOCK, G, hd)
    vb = jnp.pad(v, kpad).reshape(B, nb + n_prev, BAND_BLOCK, G, hd)
    k_band = jnp.concatenate([kb[:, j:j + nb] for j in range(n_prev + 1)], axis=2)
    v_band = jnp.concatenate([vb[:, j:j + nb] for j in range(n_prev + 1)], axis=2)
    qb = jnp.pad(q, ((0, 0), (0, lp - L), (0, 0), (0, 0))).reshape(B, nb, BAND_BLOCK, G, rep, hd)
    qi = jnp.arange(BAND_BLOCK, dtype=jnp.int32)
    dist = n_prev * BAND_BLOCK + qi[:, None] - jnp.arange(kw, dtype=jnp.int32)[None, :]
    q_abs = (jnp.arange(nb, dtype=jnp.int32)[:, None] * BAND_BLOCK + qi[None, :])[:, :, None]
    mask = (dist >= 0) & (dist <= max_back) & (q_abs >= dist)
    bias = bias_hT[:, t5_bucket(jnp.maximum(dist, 0) * dist_scale)].reshape(G, rep, BAND_BLOCK, kw)
    logits = jnp.einsum('bnqgrd,bnkgd->bngrqk', qb, k_band) * SCALE + bias
    p, m, l = masked_softmax(logits, mask[None, :, None, None])
    o = jnp.einsum('bngrqk,bnkgd->bnqgrd', p.astype(v.dtype), v_band).reshape(B, lp, H, hd)[:, :L]

    def rows(s):
        return s.transpose(0, 1, 4, 2, 3).reshape(B, lp, H)[:, :L]
    return o, rows(m), rows(l)


def gather_attend(q, k, v, q_idx, dists, bias_hT):
    B, T, H, hd = q.shape
    G = k.shape[2]
    rep = H // G
    kidx = q_idx[:, None] - dists[None, :]
    valid = kidx >= 0
    kidx = jnp.maximum(kidx, 0)
    kg = k[:, kidx]
    vg = v[:, kidx]
    bias = bias_hT[:, t5_bucket(dists)].reshape(G, rep, -1)
    logits = jnp.einsum('btgrd,btngd->btgrn', q.reshape(B, T, G, rep, hd), kg) * SCALE + bias
    p, m, l = masked_softmax(logits, valid[None, :, None, None, :])
    o = jnp.einsum('btgrn,btngd->btgrd', p.astype(v.dtype), vg)
    return o.reshape(B, T, H, hd), m.reshape(B, T, H), l.reshape(B, T, H)


def moba_attention(q, q_pos, k, v, bias_hT):
    B, L, HK, hd = k.shape
    H = q.shape[2]
    rep = H // HK
    nblk = -(-L // MOBA_BLOCK)
    pad = ((0, 0), (0, nblk * MOBA_BLOCK - L), (0, 0), (0, 0))
    kb = jnp.pad(k, pad).reshape(B, nblk, MOBA_BLOCK, HK, hd)
    vb = jnp.pad(v, pad).reshape(B, nblk, MOBA_BLOCK, HK, hd)
    k_mean = jnp.repeat(jnp.mean(kb.astype(jnp.float32), axis=2), rep, axis=2)
    kb_t = kb.transpose(0, 3, 1, 2, 4)
    vb_t = vb.transpose(0, 3, 1, 2, 4)
    n_top = min(MOBA_TOPK, nblk)
    blk_ids = jnp.arange(nblk, dtype=jnp.int32)
    within = jnp.arange(MOBA_BLOCK, dtype=jnp.int32)
    b_idx = jnp.arange(B)[:, None, None, None]
    kv_idx = (jnp.arange(H) // rep)[None, None, :, None]
    h_idx = jnp.arange(H)[None, None, :, None, None]

    def chunk(qc, pc):
        C = qc.shape[1]
        own = pc // MOBA_BLOCK
        gate = jnp.einsum('bchd,bnhd->bchn', qc.astype(jnp.float32), k_mean)
        fully_past = (blk_ids[None, :] < own[:, None])[None, :, None, :]
        _, top = lax.top_k(jnp.where(fully_past, gate, -jnp.inf), n_top)
        own_b = jnp.broadcast_to(own[None, :, None, None], (B, C, H, 1))
        blocks = jnp.concatenate([top, own_b], axis=-1)
        ok = jnp.concatenate([top < own_b, jnp.ones((B, C, H, 1), dtype=bool)], axis=-1)
        kg = kb_t[b_idx, kv_idx, blocks]
        vg = vb_t[b_idx, kv_idx, blocks]
        dist = pc[None, :, None, None, None] - (blocks[..., None] * MOBA_BLOCK + within)
        mask = ok[..., None] & (dist >= 0)
        bias = bias_hT[h_idx, t5_bucket(jnp.maximum(dist, 0))]
        logits = jnp.einsum('bchd,bchjkd->bchjk', qc, kg) * SCALE + bias
        n = logits.shape[3] * MOBA_BLOCK
        p, _, _ = masked_softmax(logits.reshape(B, C, H, n), mask.reshape(B, C, H, n))
        return jnp.einsum('bchn,bchnd->bchd', p.astype(v.dtype), vg.reshape(B, C, H, n, hd))
    return map_query_chunks(chunk, q, q_pos)


def nsa_compress(x, pe, w1, w2):
    B, L, G, hd = x.shape
    n_cmp = (L - CMP_LEN) // CMP_STRIDE + 1
    idx = jnp.arange(n_cmp)[:, None] * CMP_STRIDE + jnp.arange(CMP_LEN)[None, :]
    blocks = x[:, idx] + pe[:, None, :]
    flat = blocks.transpose(0, 1, 3, 2, 4).reshape(B, n_cmp, G, CMP_LEN * hd)
    return jax.nn.silu(flat @ w1) @ w2


def nsa_sparse(q, q_pos, kc, vc, ks, vs, pe, w1, w2, bias_hT):
    B, L, G, hd = ks.shape
    H = q.shape[2]
    rep = H // G
    k_cmp = nsa_compress(kc, pe[0], w1[0], w2[0])
    v_cmp = nsa_compress(vc, pe[1], w1[1], w2[1])
    n_cmp = k_cmp.shape[1]
    c_start = jnp.arange(n_cmp, dtype=jnp.int32) * CMP_STRIDE
    c_end = c_start + CMP_LEN - 1
    n_slc = -(-L // SLC_BLOCK)
    slc_ids = jnp.arange(n_slc, dtype=jnp.int32)
    cover = ((c_start[:, None] < (slc_ids[None, :] + 1) * SLC_BLOCK)
             & (c_start[:, None] + CMP_LEN > slc_ids[None, :] * SLC_BLOCK)).astype(jnp.float32)
    pad = ((0, 0), (0, n_slc * SLC_BLOCK - L), (0, 0), (0, 0))
    ksb = jnp.pad(ks, pad).reshape(B, n_slc, SLC_BLOCK, G, hd).transpose(0, 3, 1, 2, 4)
    vsb = jnp.pad(vs, pad).reshape(B, n_slc, SLC_BLOCK, G, hd).transpose(0, 3, 1, 2, 4)
    n_top = min(SLC_TOPK, n_slc)
    within = jnp.arange(SLC_BLOCK, dtype=jnp.int32)
    b_idx = jnp.arange(B)[:, None, None, None]
    g_idx = jnp.arange(G)[None, None, :, None]
    g6 = jnp.arange(G)[None, None, :, None, None, None]
    r6 = jnp.arange(rep)[None, None, None, :, None, None]
    bias_g = bias_hT.reshape(G, rep, N_BUCKETS)

    def chunk(qc, pc):
        C = qc.shape[1]
        qg = qc.reshape(B, C, G, rep, hd)
        lc = jnp.einsum('bcgrd,bngd->bcgrn', qg, k_cmp) * SCALE
        mc = (c_end[None, :] <= pc[:, None])[None, :, None, None, :]
        p_cmp, _, _ = masked_softmax(lc, mc)
        o_cmp = jnp.einsum('bcgrn,bngd->bcgrd', p_cmp.astype(vc.dtype), v_cmp)
        imp = jnp.einsum('bcgrn,ns->bcgs', p_cmp, cover)
        own = pc // SLC_BLOCK
        future = (slc_ids[None, :] > own[:, None])[None, :, None, :]
        forced = ((slc_ids[None, :] == 0) | (slc_ids[None, :] == own[:, None])
                  | (slc_ids[None, :] == own[:, None] - 1))[None, :, None, :]
        score = jnp.where(forced, jnp.inf, jnp.where(future, -jnp.inf, imp))
        _, top = lax.top_k(score, n_top)
        kg = ksb[b_idx, g_idx, top]
        vg = vsb[b_idx, g_idx, top]
        dist = pc[None, :, None, None, None] - (top[..., None] * SLC_BLOCK + within)
        bias = bias_g[g6, r6, t5_bucket(jnp.maximum(dist, 0))[:, :, :, None]]
        ls = jnp.einsum('bcgrd,bcgjkd->bcgrjk', qg, kg) * SCALE + bias
        n = ls.shape[4] * SLC_BLOCK
        ms = jnp.broadcast_to((dist >= 0)[:, :, :, None], ls.shape).reshape(B, C, G, rep, n)
        p_sel, _, _ = masked_softmax(ls.reshape(B, C, G, rep, n), ms)
        o_sel = jnp.einsum('bcgrn,bcgnd->bcgrd', p_sel.astype(vs.dtype), vg.reshape(B, C, G, n, hd))
        return o_cmp.reshape(B, C, H, hd), o_sel.reshape(B, C, H, hd)
    return map_query_chunks(chunk, q, q_pos)


def merge_ab(o_a, o_cmp, o_sel, o_win, gate, w_out):
    g = jax.nn.sigmoid(gate.astype(jnp.float32)).astype(o_cmp.dtype)
    o_b = g[..., 0:1] * o_cmp + g[..., 1:2] * o_sel + g[..., 2:3] * o_win
    o = jnp.concatenate([o_a, o_b], axis=2)
    return o.reshape(o.shape[:2] + (-1,)) @ w_out


def ab_mixer_prompt(h, w_in, w_out, pe, w1, w2, bias_a, bias_b):
    B, S, _ = h.shape
    pos = jnp.arange(S, dtype=jnp.int32)
    qa, ka, va, qb, kc, vc, ks, vs, kw, vw, gate = split_heads(h @ w_in, AB_SPLITS, AB_HEADS)
    o_a = moba_attention(qa, pos, ka, va, bias_a)
    o_cmp, o_sel = nsa_sparse(qb, pos, kc, vc, ks, vs, pe, w1, w2, bias_b)
    o_win = band_attention(qb, kw, vw, WIN_B, 1, bias_b)[0]
    y = merge_ab(o_a, o_cmp, o_sel, o_win, gate, w_out)
    nw = min(WIN_B, S)
    return y, jnp.stack([ka, va], 2), jnp.stack([kc, vc, ks, vs], 2), jnp.stack([kw, vw], 2)[:, S - nw:]


def ab_mixer_sample(h, cache_a, cache_b, win_buf, page_table, w_in, w_out, pe, w1, w2, bias_a, bias_b):
    DB, T, _ = h.shape
    past = page_table.shape[1] * cache_a.shape[1]
    pos = past + jnp.arange(T, dtype=jnp.int32)
    qa, ka, va, qb, kc, vc, ks, vs, kw, vw, gate = split_heads(h @ w_in, AB_SPLITS, AB_HEADS)
    pa = cache_a[page_table].reshape((DB, past) + cache_a.shape[2:])
    o_a = moba_attention(qa, pos, jnp.concatenate([pa[:, :, 0], ka], 1), jnp.concatenate([pa[:, :, 1], va], 1), bias_a)
    pb = cache_b[page_table].reshape((DB, past) + cache_b.shape[2:])
    o_cmp, o_sel = nsa_sparse(qb, pos,
                              jnp.concatenate([pb[:, :, 0], kc], 1), jnp.concatenate([pb[:, :, 1], vc], 1),
                              jnp.concatenate([pb[:, :, 2], ks], 1), jnp.concatenate([pb[:, :, 3], vs], 1),
                              pe, w1, w2, bias_b)
    wb = win_buf.shape[1]
    o_win = gather_attend(qb, jnp.concatenate([win_buf[:, :, 0], kw], 1), jnp.concatenate([win_buf[:, :, 1], vw], 1),
                          wb + jnp.arange(T, dtype=jnp.int32), jnp.arange(WIN_B + 1, dtype=jnp.int32), bias_b)[0]
    y = merge_ab(o_a, o_cmp, o_sel, o_win, gate, w_out)
    return y, jnp.stack([ka, va], 2), jnp.stack([kc, vc, ks, vs], 2), jnp.stack([kw, vw], 2)


def combine_by_denominator(outs):
    ms = jnp.stack([m for _, m, _ in outs])
    top = jnp.max(ms, axis=0)
    ws = jnp.stack([l * jnp.exp(m - top) for _, m, l in outs])
    os_ = jnp.stack([o.astype(jnp.float32) for o, _, _ in outs])
    return jnp.sum(ws[..., None] * os_, axis=0) / jnp.sum(ws, axis=0)[..., None]


def dilated_prompt(q, k, v, win, dil, bias_hT):
    B, S, H, hd = q.shape

    def fold(x):
        return x.reshape(B, S // dil, dil, x.shape[2], hd).transpose(0, 2, 1, 3, 4).reshape(B * dil, S // dil, x.shape[2], hd)
    o, m, l = band_attention(fold(q), fold(k), fold(v), win // dil, dil, bias_hT)
    o = o.reshape(B, dil, S // dil, H, hd).transpose(0, 2, 1, 3, 4).reshape(B, S, H, hd)

    def unfold(s):
        return s.reshape(B, dil, S // dil, H).transpose(0, 2, 1, 3).reshape(B, S, H)
    return o, unfold(m), unfold(l)


def c_mixer_prompt(h, w_in, w_out, bias_c):
    B, S, _ = h.shape
    q, k, v = split_heads(h @ w_in, C_SPLITS, C_HEADS)
    o = combine_by_denominator([dilated_prompt(q, k, v, w, d, bias_c) for w, d in C_PATTERNS]).astype(h.dtype)
    nc = min(WIN_C, S)
    return o.reshape(B, S, -1) @ w_out, jnp.stack([k, v], 2)[:, S - nc:]


def c_mixer_sample(h, buf, w_in, w_out, bias_c):
    DB, T, _ = h.shape
    q, k, v = split_heads(h @ w_in, C_SPLITS, C_HEADS)
    wc = buf.shape[1]
    kseq = jnp.concatenate([buf[:, :, 0], k], 1)
    vseq = jnp.concatenate([buf[:, :, 1], v], 1)
    q_idx = wc + jnp.arange(T, dtype=jnp.int32)
    outs = [gather_attend(q, kseq, vseq, q_idx, jnp.arange(w // d + 1, dtype=jnp.int32) * d, bias_c) for w, d in C_PATTERNS]
    o = combine_by_denominator(outs).astype(h.dtype)
    return o.reshape(DB, T, -1) @ w_out, jnp.stack([k, v], 2)


def setup_inputs(seed: int = 0) -> dict:
    key = jax.random.key(seed)
    k = jax.random.split(key, 20)
    n_even, n_odd = (DEPTH + 1) // 2, DEPTH // 2
    n_pages = PAST_LEN // PAGE_SIZE
    n_pool = (5 * DEC_BATCH * n_pages + 3) // 4
    wb, wc = min(WIN_B, PAST_LEN), min(WIN_C, PAST_LEN)
    p_ab = sum(AB_SPLITS)

    def nrm(kk, shape, scale):
        return scale * jax.random.normal(kk, shape, jnp.float32)
    page_table = jax.random.permutation(k[6], n_pool)[:DEC_BATCH * n_pages].reshape(DEC_BATCH, n_pages).astype(jnp.int32)
    return {
        'x_prompt': nrm(k[0], (BATCH, SEQ, D_MODEL), 1.0),
        'x_sample': nrm(k[1], (DEC_BATCH, DEC_SEQ, D_MODEL), 1.0),
        'cache_a_kv': nrm(k[2], (n_even, n_pool, PAGE_SIZE, 2, H_KV_A, HEAD_DIM), 1.0),
        'cache_b_kv': nrm(k[3], (n_even, n_pool, PAGE_SIZE, 4, G_B, HEAD_DIM), 1.0),
        'state_b_win': nrm(k[4], (n_even, DEC_BATCH, wb, 2, G_B, HEAD_DIM), 1.0),
        'state_c_kv': nrm(k[5], (n_odd, DEC_BATCH, wc, 2, H_C, HEAD_DIM), 1.0),
        'page_table': page_table,
        't5_bias': nrm(k[7], (N_BUCKETS, N_HEADS), 0.3),
        'norm_g': 1.0 + nrm(k[8], (DEPTH, 6, D_MODEL), 0.05),
        'w_ffn_in': nrm(k[9], (DEPTH, 2, D_MODEL, 2 * D_FF), D_MODEL ** -0.5),
        'w_ffn_out': nrm(k[10], (DEPTH, 2, D_FF, D_MODEL), D_FF ** -0.5),
        'w_in_ab': nrm(k[11], (n_even, D_MODEL, p_ab), D_MODEL ** -0.5),
        'w_out_ab': nrm(k[12], (n_even, N_HEADS * HEAD_DIM, D_MODEL), (N_HEADS * HEAD_DIM) ** -0.5),
        'cmp_pe': nrm(k[13], (n_even, 2, CMP_LEN, HEAD_DIM), 0.5),
        'w_cmp_1': nrm(k[14], (n_even, 2, CMP_LEN * HEAD_DIM, CMP_HID), (CMP_LEN * HEAD_DIM) ** -0.5),
        'w_cmp_2': nrm(k[15], (n_even, 2, CMP_HID, HEAD_DIM), CMP_HID ** -0.5),
        'w_in_c': nrm(k[16], (n_odd, D_MODEL, sum(C_SPLITS)), D_MODEL ** -0.5),
        'w_out_c': nrm(k[17], (n_odd, H_C * HEAD_DIM, D_MODEL), (H_C * HEAD_DIM) ** -0.5),
    }


def reference(x_prompt, x_sample, cache_a_kv, cache_b_kv, state_b_win, state_c_kv, page_table,
              t5_bias, norm_g, w_ffn_in, w_ffn_out, w_in_ab, w_out_ab, cmp_pe, w_cmp_1, w_cmp_2,
              w_in_c, w_out_c):
    bias_a = t5_bias[:, :H_A].T
    bias_b = t5_bias[:, H_A:H_A + H_B].T
    bias_c = t5_bias[:, :H_C].T
    xp, xs = x_prompt, x_sample
    a_p, a_s, b_p, b_s, w_p, w_s, c_p, c_s = [], [], [], [], [], [], [], []
    for layer in range(DEPTH):
        i = layer // 2
        g = norm_g[layer]
        xp = macaron_half(xp, g[0], g[1], w_ffn_in[layer, 0], w_ffn_out[layer, 0])
        xs = macaron_half(xs, g[0], g[1], w_ffn_in[layer, 0], w_ffn_out[layer, 0])
        hp, hs = rms_norm(xp, g[2]), rms_norm(xs, g[2])
        if layer % 2 == 0:
            yp, na, nb, nw = ab_mixer_prompt(hp, w_in_ab[i], w_out_ab[i], cmp_pe[i], w_cmp_1[i], w_cmp_2[i], bias_a, bias_b)
            ys, sa, sb, sw = ab_mixer_sample(hs, cache_a_kv[i], cache_b_kv[i], state_b_win[i], page_table,
                                             w_in_ab[i], w_out_ab[i], cmp_pe[i], w_cmp_1[i], w_cmp_2[i], bias_a, bias_b)
            a_p.append(na)
            a_s.append(sa)
            b_p.append(nb)
            b_s.append(sb)
            w_p.append(nw)
            w_s.append(sw)
        else:
            yp, nc = c_mixer_prompt(hp, w_in_c[i], w_out_c[i], bias_c)
            ys, sc = c_mixer_sample(hs, state_c_kv[i], w_in_c[i], w_out_c[i], bias_c)
            c_p.append(nc)
            c_s.append(sc)
        xp = xp + rms_norm(yp, g[3])
        xs = xs + rms_norm(ys, g[3])
        xp = macaron_half(xp, g[4], g[5], w_ffn_in[layer, 1], w_ffn_out[layer, 1])
        xs = macaron_half(xs, g[4], g[5], w_ffn_in[layer, 1], w_ffn_out[layer, 1])
    return (xp, xs, jnp.stack(a_p), jnp.stack(a_s), jnp.stack(b_p), jnp.stack(b_s),
            jnp.stack(w_p), jnp.stack(w_s), jnp.stack(c_p), jnp.stack(c_s))
```

```python
import functools
import math

import numpy as np
import jax
import jax.numpy as jnp
from jax import lax
from jax.experimental import pallas as pl
from jax.experimental.pallas import tpu as pltpu

F32 = jnp.float32
MXU_DTYPE = jnp.bfloat16
HI = lax.Precision.HIGHEST
NEG = -1e30
TINY = 1e-30
EPS = 1e-6
LANES = 128
VMEM_LIMIT = 56 * 1024 * 1024

HEAD_DIM = 128
SCALE = HEAD_DIM ** -0.5
N_BUCKETS = 32
T5_MAX_DIST = 128
H_A, H_KV_A, H_B, G_B, H_C = 8, 4, 8, 2, 16
REP_A = H_A // H_KV_A
REP_B = H_B // G_B
MOBA_BLOCK, MOBA_TOPK = 256, 3
CMP_LEN, CMP_STRIDE = 32, 16
SLC_BLOCK, SLC_TOPK = 64, 16
WIN_B = 512
C_PATTERNS = ((128, 1), (512, 4), (2048, 16))
PAGE = 128
TQ = 256
BAND = 128

A_Q, A_K, A_V, B_Q = 0, 1024, 1536, 2048
B_KC, B_VC, B_KS, B_VS, B_KW, B_VW, B_GATE = 3072, 3328, 3584, 3840, 4096, 4352, 4608
P_AB = 4632
P_AB_PAD = 5120
P_C = 6144

NT = (((1,), (1,)), ((), ()))


def _cparams(sem):
    return pltpu.CompilerParams(dimension_semantics=sem, vmem_limit_bytes=VMEM_LIMIT)


def _log2(n):
    assert n & (n - 1) == 0
    return n.bit_length() - 1


def _rms(x, g):
    return x * lax.rsqrt(jnp.mean(x * x, axis=-1, keepdims=True) + EPS) * g


def _ffn_kernel(x_ref, gpre_ref, gpost_ref, wg_ref, wu_ref, wo_ref, o_ref, h_sc, acc_sc):
    j = pl.program_id(1)

    @pl.when(j == 0)
    def _():
        h_sc[...] = _rms(x_ref[...], gpre_ref[...]).astype(h_sc.dtype)
        acc_sc[...] = jnp.zeros_like(acc_sc)

    h = h_sc[...]
    g = jnp.dot(h, wg_ref[...], preferred_element_type=F32)
    u = jnp.dot(h, wu_ref[...], preferred_element_type=F32)
    a = g * jax.nn.sigmoid(g) * u
    acc_sc[...] += jnp.dot(a.astype(MXU_DTYPE), wo_ref[...], preferred_element_type=F32)

    @pl.when(j == pl.num_programs(1) - 1)
    def _():
        o_ref[...] = x_ref[...] + 0.5 * _rms(acc_sc[...], gpost_ref[...])


def _ffn_half(x, g_pre, g_post, w_in_p, w_out_p, tm, tf=512):
    n, d = x.shape
    ffp = w_out_p.shape[0]
    nff = ffp // tf
    return pl.pallas_call(
        _ffn_kernel,
        out_shape=jax.ShapeDtypeStruct((n, d), F32),
        grid=(n // tm, nff),
        in_specs=[
            pl.BlockSpec((tm, d), lambda i, j: (i, 0)),
            pl.BlockSpec((1, d), lambda i, j: (0, 0)),
            pl.BlockSpec((1, d), lambda i, j: (0, 0)),
            pl.BlockSpec((d, tf), lambda i, j: (0, j)),
            pl.BlockSpec((d, tf), lambda i, j: (0, j + nff)),
            pl.BlockSpec((tf, d), lambda i, j: (j, 0)),
        ],
        out_specs=pl.BlockSpec((tm, d), lambda i, j: (i, 0)),
        scratch_shapes=[pltpu.VMEM((tm, d), MXU_DTYPE), pltpu.VMEM((tm, d), F32)],
        compiler_params=_cparams(("parallel", "arbitrary")),
        name="ffn_half",
    )(x, g_pre.reshape(1, d), g_post.reshape(1, d), w_in_p, w_in_p, w_out_p)


def _norm_mm_kernel(x_ref, g_ref, w_ref, o_ref, h_sc):
    @pl.when(pl.program_id(1) == 0)
    def _():
        h_sc[...] = _rms(x_ref[...], g_ref[...]).astype(h_sc.dtype)

    o_ref[...] = jnp.dot(h_sc[...], w_ref[...], preferred_element_type=F32)


def _norm_matmul(x, g, w, tm, tn=1024):
    n, d = x.shape
    p = w.shape[1]
    return pl.pallas_call(
        _norm_mm_kernel,
        out_shape=jax.ShapeDtypeStruct((n, p), F32),
        grid=(n // tm, p // tn),
        in_specs=[
            pl.BlockSpec((tm, d), lambda i, j: (i, 0)),
            pl.BlockSpec((1, d), lambda i, j: (0, 0)),
            pl.BlockSpec((d, tn), lambda i, j: (0, j)),
        ],
        out_specs=pl.BlockSpec((tm, tn), lambda i, j: (i, j)),
        scratch_shapes=[pltpu.VMEM((tm, d), MXU_DTYPE)],
        compiler_params=_cparams(("parallel", "arbitrary")),
        name="norm_matmul",
    )(x, g.reshape(1, d), w)


def _mm_norm_res_kernel(x_ref, o1_ref, o2_ref, w1_ref, w2_ref, g_ref, out_ref):
    y = jnp.dot(o1_ref[...].astype(MXU_DTYPE), w1_ref[...], preferred_element_type=F32)
    y = y + jnp.dot(o2_ref[...].astype(MXU_DTYPE), w2_ref[...], preferred_element_type=F32)
    out_ref[...] = x_ref[...] + _rms(y, g_ref[...])


def _out_proj(x, o1, o2, o2_col, w, g, tm):
    n, d = x.shape
    half = d // 2
    return pl.pallas_call(
        _mm_norm_res_kernel,
        out_shape=jax.ShapeDtypeStruct((n, d), F32),
        grid=(n // tm,),
        in_specs=[
            pl.BlockSpec((tm, d), lambda i: (i, 0)),
            pl.BlockSpec((tm, half), lambda i: (i, 0)),
            pl.BlockSpec((tm, half), lambda i: (i, o2_col)),
            pl.BlockSpec((half, d), lambda i: (0, 0)),
            pl.BlockSpec((half, d), lambda i: (1, 0)),
            pl.BlockSpec((1, d), lambda i: (0, 0)),
        ],
        out_specs=pl.BlockSpec((tm, d), lambda i: (i, 0)),
        compiler_params=_cparams(("parallel",)),
        name="out_proj",
    )(x, o1, o2, w, w, g.reshape(1, d))


def _bucket_np(dist):
    dist = np.asarray(dist)
    exact = N_BUCKETS // 2
    d = np.maximum(dist, 1).astype(np.float32)
    far = exact + (np.log(d / np.float32(exact)) / np.float32(math.log(T5_MAX_DIST / exact))
                   * np.float32(N_BUCKETS - exact)).astype(np.int32)
    return np.where(dist < exact, dist, np.minimum(far, N_BUCKETS - 1)).astype(np.int32)


def _bias_table(t5_heads, dist, valid):
    idx = _bucket_np(np.maximum(dist, 0))
    tab = jnp.moveaxis(t5_heads[idx], -1, 0)
    return jnp.where(jnp.asarray(valid)[None], tab, NEG)


def _toeplitz_dist(rows, cols, offset):
    return offset + np.arange(rows)[:, None] - np.arange(cols)[None, :]


def _flash_init(m_sc, l_sc, acc_sc):
    m_sc[...] = jnp.full(m_sc.shape, NEG, F32)
    l_sc[...] = jnp.zeros_like(l_sc)
    acc_sc[...] = jnp.zeros_like(acc_sc)


def _flash_update(s, v, m_sc, l_sc, acc_sc):
    m_prev = m_sc[...]
    m_new = jnp.maximum(m_prev, jnp.max(s, axis=-1, keepdims=True))
    alpha = jnp.exp(m_prev - m_new)
    p = jnp.exp(s - m_new)
    l_sc[...] = alpha * l_sc[...] + jnp.sum(p, axis=-1, keepdims=True)
    acc_sc[...] = alpha * acc_sc[...] + jnp.dot(p.astype(MXU_DTYPE), v, preferred_element_type=F32)
    m_sc[...] = m_new


def _stack_heads(q, n):
    return jnp.concatenate([q[:, r * HEAD_DIM:(r + 1) * HEAD_DIM] for r in range(n)], axis=0)


def _unstack_heads(o, n):
    t = o.shape[0] // n
    return jnp.concatenate([o[r * t:(r + 1) * t] for r in range(n)], axis=1)


def _rank_lt(score, idx, axis, n_items, k):
    sel = jnp.zeros(score.shape, F32)
    for n in range(n_items):
        sn = score[:, n:n + 1] if axis == 1 else score[n:n + 1, :]
        beats = (score > sn) | ((score == sn) & (idx < n))
        rank = jnp.sum(beats.astype(F32), axis=axis, keepdims=True)
        sel = jnp.where((idx == n) & (rank < k), 1.0, sel)
    return sel


def _moba_prompt_kernel(q_ref, k_ref, v_ref, tab_ref, o_ref, kmean_sc, m_sc, l_sc, acc_sc, *, nblk):
    qi = pl.program_id(2)

    @pl.when(qi == 0)
    def _():
        kmean_sc[...] = jnp.zeros_like(kmean_sc)
        for n in range(nblk):
            blk = k_ref[0, n * MOBA_BLOCK:(n + 1) * MOBA_BLOCK, :]
            kmean_sc[n:n + 1, :] = jnp.sum(blk, axis=0, keepdims=True) * (1.0 / MOBA_BLOCK)

    qs = _stack_heads(q_ref[0], REP_A)
    gate = lax.dot_general(qs, kmean_sc[...], NT, precision=HI, preferred_element_type=F32)
    lane = lax.broadcasted_iota(jnp.int32, gate.shape, 1)
    gate = jnp.where(lane < qi, gate, -jnp.inf)
    sel = _rank_lt(gate, lane, 1, nblk, MOBA_TOPK)
    sel = jnp.where(lane < qi, sel, 0.0)

    qb = qs.astype(MXU_DTYPE)
    _flash_init(m_sc, l_sc, acc_sc)

    def tile(j, kind, rowmask):
        start = pl.multiple_of(j * MOBA_BLOCK, MOBA_BLOCK)
        k = k_ref[0, pl.ds(start, MOBA_BLOCK), :].astype(MXU_DTYPE)
        v = v_ref[0, pl.ds(start, MOBA_BLOCK), :].astype(MXU_DTYPE)
        s = lax.dot_general(qb, k, NT, preferred_element_type=F32) * SCALE + tab_ref[0, kind]
        if rowmask is not None:
            s = jnp.where(rowmask > 0.5, s, NEG)
        _flash_update(s, v, m_sc, l_sc, acc_sc)

    tile(qi, 0, None)

    def body(j, c):
        sel_j = jnp.sum(jnp.where(lane == j, sel, 0.0), axis=1, keepdims=True)
        tile(j, jnp.minimum(qi - j, 2), sel_j)
        return c

    lax.fori_loop(0, qi, body, 0)
    o_ref[0] = _unstack_heads(acc_sc[...] / l_sc[...], REP_A)


def _moba_prompt(z3, tab):
    b, s, _ = z3.shape
    nq = s // TQ
    rows = REP_A * TQ
    return pl.pallas_call(
        functools.partial(_moba_prompt_kernel, nblk=s // MOBA_BLOCK),
        out_shape=jax.ShapeDtypeStruct((b, s, H_A * HEAD_DIM), F32),
        grid=(b, H_KV_A, nq),
        in_specs=[
            pl.BlockSpec((1, TQ, REP_A * HEAD_DIM), lambda bi, h, qi: (bi, qi, h)),
            pl.BlockSpec((1, s, HEAD_DIM), lambda bi, h, qi: (bi, 0, A_K // HEAD_DIM + h)),
            pl.BlockSpec((1, s, HEAD_DIM), lambda bi, h, qi: (bi, 0, A_V // HEAD_DIM + h)),
            pl.BlockSpec((1, 3, rows, TQ), lambda bi, h, qi: (h, 0, 0, 0)),
        ],
        out_specs=pl.BlockSpec((1, TQ, REP_A * HEAD_DIM), lambda bi, h, qi: (bi, qi, h)),
        scratch_shapes=[pltpu.VMEM((LANES, HEAD_DIM), F32), pltpu.VMEM((rows, 1), F32),
                        pltpu.VMEM((rows, 1), F32), pltpu.VMEM((rows, HEAD_DIM), F32)],
        compiler_params=_cparams(("parallel", "parallel", "arbitrary")),
        name="moba_prompt",
    )(z3, z3, z3, tab)


def _prompt_tables(t5_heads, n_groups, rep, kinds):
    tabs = []
    for offset, max_back, const in kinds:
        if const:
            dist = np.full((TQ, TQ), T5_MAX_DIST)
            valid = np.ones((TQ, TQ), bool)
        else:
            dist = _toeplitz_dist(TQ, TQ, offset)
            valid = dist >= 0
            if max_back is not None:
                valid &= dist <= max_back
        tabs.append(_bias_table(t5_heads, dist, valid))
    t = jnp.stack(tabs, axis=1)
    t = t.reshape(n_groups, rep, len(kinds), TQ, TQ).transpose(0, 2, 1, 3, 4)
    return t.reshape(n_groups, len(kinds), rep * TQ, TQ)


def _compress_seq(x_refs, w1_ref, w2_ref, pe_ref, slot):
    n_rows = 128
    ng = len(x_refs)
    acc_a = jnp.zeros((ng * n_rows, HEAD_DIM), F32)
    acc_b = jnp.zeros((ng * n_rows, HEAD_DIM), F32)
    for u in range(CMP_STRIDE):
        xs = jnp.concatenate([x[pl.ds(u, n_rows, stride=CMP_STRIDE), :] for x in x_refs], axis=0)
        xs = xs.astype(MXU_DTYPE)
        acc_a += jnp.dot(xs, w1_ref[slot, u * HEAD_DIM:(u + 1) * HEAD_DIM, :], preferred_element_type=F32)
        acc_b += jnp.dot(xs, w1_ref[slot, (CMP_STRIDE + u) * HEAD_DIM:(CMP_STRIDE + u + 1) * HEAD_DIM, :],
                         preferred_element_type=F32)
    pe8 = jnp.broadcast_to(pe_ref[slot], (8, CMP_LEN * HEAD_DIM)).astype(MXU_DTYPE)
    c = jnp.dot(pe8, w1_ref[slot], preferred_element_type=F32)[0:1]
    outs = []
    for g in range(ng):
        a = acc_a[g * n_rows:(g + 1) * n_rows]
        bsh = pltpu.roll(acc_b[g * n_rows:(g + 1) * n_rows], n_rows - 1, 0)
        y = a + bsh + c
        hid = (y * jax.nn.sigmoid(y)).astype(MXU_DTYPE)
        outs.append(jnp.dot(hid, w2_ref[slot], preferred_element_type=F32))
    return jnp.concatenate(outs, axis=1) if ng > 1 else outs[0]


def _cover_matrix(n_cmp, n_slc, transposed):
    shape = (LANES, LANES)
    n = lax.broadcasted_iota(jnp.int32, shape, 1 if transposed else 0)
    s = lax.broadcasted_iota(jnp.int32, shape, 0 if transposed else 1)
    c = ((n * CMP_STRIDE < (s + 1) * SLC_BLOCK) & (n * CMP_STRIDE + CMP_LEN > s * SLC_BLOCK)
         & (n < n_cmp) & (s < n_slc))
    return c.astype(F32)


def _nsa_prompt_kernel(q_ref, kc_ref, vc_ref, ks_ref, vs_ref, kw_ref, vw_ref, gate_ref, pe_ref, w1_ref,
                       w2_ref, tabs_ref, tabw_ref, o_ref, kcmp_sc, vcmp_sc, m_sc, l_sc, acc_sc,
                       *, n_cmp, n_slc):
    g = pl.program_id(1)
    qi = pl.program_id(2)

    @pl.when(qi == 0)
    def _():
        kcmp_sc[...] = _compress_seq([kc_ref.at[0]], w1_ref, w2_ref, pe_ref, 0)
        vcmp_sc[...] = _compress_seq([vc_ref.at[0]], w1_ref, w2_ref, pe_ref, 1)

    rows = REP_B * TQ
    qs = _stack_heads(q_ref[0], REP_B)
    qb = qs.astype(MXU_DTYPE)
    t_tile = qi * TQ + lax.broadcasted_iota(jnp.int32, (TQ, LANES), 0)
    lane = lax.broadcasted_iota(jnp.int32, (TQ, LANES), 1)

    s = lax.dot_general(qb, kcmp_sc[...].astype(MXU_DTYPE), NT, preferred_element_type=F32) * SCALE
    t_rows = qi * TQ + (lax.broadcasted_iota(jnp.int32, (rows, LANES), 0) & (TQ - 1))
    lane_rows = lax.broadcasted_iota(jnp.int32, (rows, LANES), 1)
    ok = (lane_rows * CMP_STRIDE + CMP_LEN - 1 <= t_rows) & (lane_rows < n_cmp)
    s = jnp.where(ok, s, NEG)
    m = jnp.max(s, axis=-1, keepdims=True)
    e = jnp.where(ok, jnp.exp(s - m), 0.0)
    p = e / jnp.maximum(jnp.sum(e, axis=-1, keepdims=True), TINY)
    o_cmp = jnp.dot(p.astype(MXU_DTYPE), vcmp_sc[...].astype(MXU_DTYPE), preferred_element_type=F32)

    psum = p[0:TQ]
    for r in range(1, REP_B):
        psum = psum + p[r * TQ:(r + 1) * TQ]
    imp = jnp.dot(psum, _cover_matrix(n_cmp, n_slc, False), precision=HI, preferred_element_type=F32)
    own = t_tile >> _log2(SLC_BLOCK)
    forced = (lane == 0) | (lane == own) | (lane == own - 1)
    score = jnp.where(forced, 1e30, jnp.where(lane > own, -1e30, imp))
    score = jnp.where(lane < n_slc, score, -2e30)
    sel = _rank_lt(score, lane, 1, n_slc, SLC_TOPK).astype(MXU_DTYPE)

    _flash_init(m_sc, l_sc, acc_sc)
    blk_of_lane = lax.broadcasted_iota(jnp.int32, (LANES, TQ), 1) >> _log2(SLC_BLOCK)
    blk_row = lax.broadcasted_iota(jnp.int32, (LANES, TQ), 0)

    def sel_tile(j, kind):
        start = pl.multiple_of(j * TQ, TQ)
        k = ks_ref[0, pl.ds(start, TQ), :].astype(MXU_DTYPE)
        v = vs_ref[0, pl.ds(start, TQ), :].astype(MXU_DTYPE)
        expand = (blk_row == blk_of_lane + j * (TQ // SLC_BLOCK)).astype(MXU_DTYPE)
        keep1 = jnp.dot(sel, expand, preferred_element_type=F32)
        keep = jnp.concatenate([keep1] * REP_B, axis=0) > 0.5
        sc = lax.dot_general(qb, k, NT, preferred_element_type=F32) * SCALE + tabs_ref[0, kind]
        _flash_update(jnp.where(keep, sc, NEG), v, m_sc, l_sc, acc_sc)

    sel_tile(qi, 0)

    def sel_body(j, c):
        sel_tile(j, jnp.minimum(qi - j, 2))
        return c

    lax.fori_loop(0, qi, sel_body, 0)
    o_sel = acc_sc[...] / l_sc[...]

    _flash_init(m_sc, l_sc, acc_sc)

    def win_tile(j, kind):
        start = pl.multiple_of(j * TQ, TQ)
        k = kw_ref[0, pl.ds(start, TQ), :].astype(MXU_DTYPE)
        v = vw_ref[0, pl.ds(start, TQ), :].astype(MXU_DTYPE)
        sc = lax.dot_general(qb, k, NT, preferred_element_type=F32) * SCALE + tabw_ref[0, kind]
        _flash_update(sc, v, m_sc, l_sc, acc_sc)

    win_tile(qi, 0)

    def win_body(d, c):
        win_tile(qi - d, d)
        return c

    lax.fori_loop(1, jnp.minimum(qi, WIN_B // TQ) + 1, win_body, 0)
    o_win = acc_sc[...] / l_sc[...]

    sg = jax.nn.sigmoid(gate_ref[0])
    outs = []
    for r in range(REP_B):
        base = (g * REP_B + r) * 3
        mix = jnp.zeros((TQ, HEAD_DIM), F32)
        for c, o in enumerate((o_cmp, o_sel, o_win)):
            gc = jnp.sum(jnp.where(lane == base + c, sg, 0.0), axis=1, keepdims=True)
            mix = mix + gc * o[r * TQ:(r + 1) * TQ]
        outs.append(mix)
    o_ref[0] = jnp.concatenate(outs, axis=1)


def _nsa_prompt(z3, pe_flat, w1, w2, tab_sel, tab_win):
    b, s, _ = z3.shape
    nq = s // TQ
    rows = REP_B * TQ
    n_cmp = (s - CMP_LEN) // CMP_STRIDE + 1
    n_slc = -(-s // SLC_BLOCK)

    def seq_spec(col):
        return pl.BlockSpec((1, s, HEAD_DIM), lambda bi, g, qi: (bi, 0, col // HEAD_DIM + g))

    def full(a):
        return pl.BlockSpec(a.shape, lambda bi, g, qi: (0,) * a.ndim)

    qw = REP_B * HEAD_DIM
    return pl.pallas_call(
        functools.partial(_nsa_prompt_kernel, n_cmp=n_cmp, n_slc=n_slc),
        out_shape=jax.ShapeDtypeStruct((b, s, H_B * HEAD_DIM), F32),
        grid=(b, G_B, nq),
        in_specs=[
            pl.BlockSpec((1, TQ, qw), lambda bi, g, qi: (bi, qi, B_Q // qw + g)),
            seq_spec(B_KC), seq_spec(B_VC), seq_spec(B_KS), seq_spec(B_VS), seq_spec(B_KW), seq_spec(B_VW),
            pl.BlockSpec((1, TQ, LANES), lambda bi, g, qi: (bi, qi, B_GATE // LANES)),
            full(pe_flat), full(w1), full(w2),
            pl.BlockSpec((1, 3, rows, TQ), lambda bi, g, qi: (g, 0, 0, 0)),
            pl.BlockSpec((1, 3, rows, TQ), lambda bi, g, qi: (g, 0, 0, 0)),
        ],
        out_specs=pl.BlockSpec((1, TQ, qw), lambda bi, g, qi: (bi, qi, g)),
        scratch_shapes=[pltpu.VMEM((LANES, HEAD_DIM), F32), pltpu.VMEM((LANES, HEAD_DIM), F32),
                        pltpu.VMEM((rows, 1), F32), pltpu.VMEM((rows, 1), F32),
                        pltpu.VMEM((rows, HEAD_DIM), F32)],
        compiler_params=_cparams(("parallel", "parallel", "arbitrary")),
        name="nsa_prompt",
    )(z3, z3, z3, z3, z3, z3, z3, z3, pe_flat, w1, w2, tab_sel, tab_win)


def _dil_prompt_kernel(q_ref, kp_ref, kc_ref, vp_ref, vc_ref, tab_ref, o_ref, m_ref, l_ref):
    qi = pl.program_id(2)
    col = lax.broadcasted_iota(jnp.int32, (BAND, 2 * BAND), 1)
    no_prev = (col < BAND) & (qi == 0)
    lane = lax.broadcasted_iota(jnp.int32, (BAND, LANES), 1)
    m_all = jnp.zeros((BAND, LANES), F32)
    l_all = jnp.zeros((BAND, LANES), F32)
    for h in range(H_C):
        hs = slice(h * HEAD_DIM, (h + 1) * HEAD_DIM)
        q = q_ref[0, :, hs].astype(MXU_DTYPE)
        k = jnp.concatenate([kp_ref[0, :, hs], kc_ref[0, :, hs]], axis=0).astype(MXU_DTYPE)
        v = jnp.concatenate([vp_ref[0, :, hs], vc_ref[0, :, hs]], axis=0).astype(MXU_DTYPE)
        s = lax.dot_general(q, k, NT, preferred_element_type=F32) * SCALE + tab_ref[h]
        s = jnp.where(no_prev, NEG, s)
        m = jnp.max(s, axis=-1, keepdims=True)
        e = jnp.exp(s - m)
        l = jnp.sum(e, axis=-1, keepdims=True)
        p = e / jnp.maximum(l, TINY)
        o_ref[0, :, hs] = jnp.dot(p.astype(MXU_DTYPE), v, preferred_element_type=F32)
        m_all = jnp.where(lane == h, m, m_all)
        l_all = jnp.where(lane == h, l, l_all)
    m_ref[0] = m_all
    l_ref[0] = l_all


def _dil_prompt(zc3, tab, dil):
    b, s, _ = zc3.shape
    sf = s // dil
    zv = zc3.reshape(b, sf, dil * P_C)
    nq = sf // BAND
    hw = H_C * HEAD_DIM

    def spec(part, prev):
        if prev:
            return pl.BlockSpec((1, BAND, hw), lambda bi, r, qi: (bi, jnp.maximum(qi - 1, 0), r * 3 + part))
        return pl.BlockSpec((1, BAND, hw), lambda bi, r, qi: (bi, qi, r * 3 + part))

    o, m, l = pl.pallas_call(
        _dil_prompt_kernel,
        out_shape=(jax.ShapeDtypeStruct((b, sf, dil * hw), F32),
                   jax.ShapeDtypeStruct((b, sf, dil * LANES), F32),
                   jax.ShapeDtypeStruct((b, sf, dil * LANES), F32)),
        grid=(b, dil, nq),
        in_specs=[spec(0, False), spec(1, True), spec(1, False), spec(2, True), spec(2, False),
                  pl.BlockSpec(tab.shape, lambda bi, r, qi: (0, 0, 0))],
        out_specs=(pl.BlockSpec((1, BAND, hw), lambda bi, r, qi: (bi, qi, r)),
                   pl.BlockSpec((1, BAND, LANES), lambda bi, r, qi: (bi, qi, r)),
                   pl.BlockSpec((1, BAND, LANES), lambda bi, r, qi: (bi, qi, r))),
        compiler_params=_cparams(("parallel", "parallel", "arbitrary")),
        name=f"dilated_prompt_d{dil}",
    )(zv, zv, zv, zv, zv, tab)
    return o.reshape(b * s, hw), m.reshape(b * s, LANES), l.reshape(b * s, LANES)


def _dil_combine_kernel(o1, o2, o3, m1, m2, m3, l1, l2, l3, out_ref):
    ms = (m1[...], m2[...], m3[...])
    top = jnp.maximum(jnp.maximum(ms[0], ms[1]), ms[2])
    ws = [l[...] * jnp.exp(m - top) for l, m in zip((l1, l2, l3), ms)]
    den = ws[0] + ws[1] + ws[2]
    for h in range(H_C):
        hs = slice(h * HEAD_DIM, (h + 1) * HEAD_DIM)
        num = sum(w[:, h:h + 1] * o[:, hs] for w, o in zip(ws, (o1, o2, o3)))
        out_ref[:, hs] = num / den[:, h:h + 1]


def _dil_combine(parts, tm=256):
    (o1, m1, l1), (o2, m2, l2), (o3, m3, l3) = parts
    n, hw = o1.shape
    big = pl.BlockSpec((tm, hw), lambda i: (i, 0))
    small = pl.BlockSpec((tm, LANES), lambda i: (i, 0))
    return pl.pallas_call(
        _dil_combine_kernel,
        out_shape=jax.ShapeDtypeStruct((n, hw), F32),
        grid=(n // tm,),
        in_specs=[big] * 3 + [small] * 6,
        out_specs=big,
        compiler_params=_cparams(("parallel",)),
        name="dilated_combine",
    )(o1, o2, o3, m1, m2, m3, l1, l2, l3)


def _dil_tables(t5_heads):
    tabs = []
    for win, dil in C_PATTERNS:
        dist = _toeplitz_dist(BAND, 2 * BAND, BAND)
        valid = (dist >= 0) & (dist <= win // dil)
        tabs.append(_bias_table(t5_heads, dist * dil, valid))
    return tabs


def _head_lanes(ng, rep, r, transposed):
    shape = (LANES, ng * HEAD_DIM) if transposed else (ng * HEAD_DIM, LANES)
    c = lax.broadcasted_iota(jnp.int32, shape, 1 if transposed else 0)
    lane = lax.broadcasted_iota(jnp.int32, shape, 0 if transposed else 1)
    return (lane == r * ng + (c >> _log2(HEAD_DIM))).astype(MXU_DTYPE)


def _q_groups(q, ng, rep):
    return [jnp.concatenate([q[:, (g * rep + r) * HEAD_DIM:(g * rep + r + 1) * HEAD_DIM] for g in range(ng)],
                            axis=1) for r in range(rep)]


def _dec_logits(k, q_rs, e_rs, precision=None):
    out = None
    for q_r, e_r in zip(q_rs, e_rs):
        prod = k * q_r
        if precision is None:
            t = jnp.dot(prod.astype(MXU_DTYPE), e_r, preferred_element_type=F32)
        else:
            t = jnp.dot(prod, e_r.astype(F32), precision=precision, preferred_element_type=F32)
        out = t if out is None else out + t
    return out


def _dec_pv(p, v, et_rs):
    pb = p.astype(MXU_DTYPE)
    return [jnp.sum(jnp.dot(pb, et, preferred_element_type=F32) * v, axis=0, keepdims=True) for et in et_rs]


def _row8(x):
    return jnp.broadcast_to(x, (8, x.shape[1]))


def _heads_out(o_rs, ng, rep):
    return jnp.concatenate([o_rs[r][:, g * HEAD_DIM:(g + 1) * HEAD_DIM] for g in range(ng) for r in range(rep)],
                           axis=1)


def _ab_sample_kernel(pt_ref, z_ref, *refs, n_pages):
    ca = refs[:n_pages]
    cb = refs[n_pages:2 * n_pages]
    (win_ref, pe_ref, w1_ref, w2_ref, taba_ref, tabs_ref, tabw_ref, self_ref, o_ref, seq_sc) = refs[2 * n_pages:]
    del pt_ref
    z = z_ref[0]
    past = n_pages * PAGE

    ga, ra = H_KV_A, REP_A
    wa = ga * HEAD_DIM
    qa = _q_groups(z[:, A_Q:A_Q + H_A * HEAD_DIM], ga, ra)
    ea = [_head_lanes(ga, ra, r, False) for r in range(ra)]
    eta = [_head_lanes(ga, ra, r, True) for r in range(ra)]
    pages_per_blk = MOBA_BLOCK // PAGE
    nblk = past // MOBA_BLOCK
    lg, ksum = [], []
    for p in range(n_pages):
        kp = ca[p][0, :, 0:wa]
        lg.append(_dec_logits(kp, qa, ea) * SCALE + taba_ref[p * PAGE:(p + 1) * PAGE, :])
        ksum.append(jnp.sum(kp, axis=0, keepdims=True))
    kmean = jnp.concatenate(
        [sum(ksum[n * pages_per_blk + i] for i in range(pages_per_blk)) * (1.0 / MOBA_BLOCK) for n in range(nblk)],
        axis=0)
    gate = _dec_logits(kmean, qa, ea, precision=HI)
    row = lax.broadcasted_iota(jnp.int32, gate.shape, 0)
    sel = _rank_lt(gate, row, 0, nblk, MOBA_TOPK)
    ka_new = _row8(z[:, A_K:A_K + wa])
    lg_self = (_dec_logits(ka_new, qa, ea) * SCALE)[0:1] + self_ref[0:1, :]
    lg = [jnp.where(sel[p // pages_per_blk:p // pages_per_blk + 1, :] > 0.5, x, NEG) for p, x in enumerate(lg)]
    m = lg_self
    for x in lg:
        m = jnp.maximum(m, jnp.max(x, axis=0, keepdims=True))
    e_self = jnp.exp(lg_self - m)
    den = e_self
    acc = [e * z[:, A_V:A_V + wa] for e in
           [jnp.dot(_row8(e_self).astype(MXU_DTYPE), et, preferred_element_type=F32)[0:1] for et in eta]]
    for p, x in enumerate(lg):
        e = jnp.exp(x - m)
        den = den + jnp.sum(e, axis=0, keepdims=True)
        pv = _dec_pv(e, ca[p][0, :, wa:2 * wa], eta)
        acc = [a + b for a, b in zip(acc, pv)]
    inv = 1.0 / den
    inv_r = [jnp.dot(_row8(inv), et.astype(F32), precision=HI, preferred_element_type=F32)[0:1] for et in eta]
    o_a = _heads_out([a * i for a, i in zip(acc, inv_r)], ga, ra)

    gb, rb = G_B, REP_B
    wb = gb * HEAD_DIM
    qb = _q_groups(z[:, B_Q:B_Q + H_B * HEAD_DIM], gb, rb)
    eb = [_head_lanes(gb, rb, r, False) for r in range(rb)]
    etb = [_head_lanes(gb, rb, r, True) for r in range(rb)]
    n_cmp = (past + 1 - CMP_LEN) // CMP_STRIDE + 1
    n_slc = -(-(past + 1) // SLC_BLOCK)
    own = past // SLC_BLOCK

    def attend(parts, k_self, v_self, self_bias):
        lself = (_dec_logits(_row8(k_self), qb, eb) * SCALE)[0:1] + self_bias
        mm = lself
        for x, _ in parts:
            mm = jnp.maximum(mm, jnp.max(x, axis=0, keepdims=True))
        es = jnp.exp(lself - mm)
        dd = es
        ac = [jnp.dot(_row8(es).astype(MXU_DTYPE), et, preferred_element_type=F32)[0:1] * v_self for et in etb]
        for x, v in parts:
            ee = jnp.exp(x - mm)
            dd = dd + jnp.sum(ee, axis=0, keepdims=True)
            ac = [a + b for a, b in zip(ac, _dec_pv(ee, v, etb))]
        iv = 1.0 / dd
        iv_r = [jnp.dot(_row8(iv), et.astype(F32), precision=HI, preferred_element_type=F32)[0:1] for et in etb]
        return [a * i for a, i in zip(ac, iv_r)]

    for p in range(n_pages):
        for c in range(2 * gb):
            seq_sc[c, p * PAGE:(p + 1) * PAGE, :] = cb[p][0, :, c * HEAD_DIM:(c + 1) * HEAD_DIM]
    kcmp = _compress_seq([seq_sc.at[g] for g in range(gb)], w1_ref, w2_ref, pe_ref, 0)
    vcmp = _compress_seq([seq_sc.at[gb + g] for g in range(gb)], w1_ref, w2_ref, pe_ref, 1)
    rowc = lax.broadcasted_iota(jnp.int32, (LANES, LANES), 0)
    okc = (rowc < n_cmp) & (rowc * CMP_STRIDE + CMP_LEN - 1 <= past)
    sc = jnp.where(okc, _dec_logits(kcmp, qb, eb) * SCALE, NEG)
    mc = jnp.max(sc, axis=0, keepdims=True)
    ec = jnp.where(okc, jnp.exp(sc - mc), 0.0)
    pc = ec / jnp.maximum(jnp.sum(ec, axis=0, keepdims=True), TINY)
    o_cmp = _dec_pv(pc, vcmp, etb)

    lane_i = lax.broadcasted_iota(jnp.int32, (LANES, LANES), 0)
    lane_o = lax.broadcasted_iota(jnp.int32, (LANES, LANES), 1)
    fold = ((lane_i < H_B) & ((lane_i & (gb - 1)) == lane_o)).astype(F32)
    unfold = ((lane_o < H_B) & ((lane_o & (gb - 1)) == lane_i)).astype(MXU_DTYPE)
    pg = jnp.dot(pc, fold, precision=HI, preferred_element_type=F32)
    imp = jnp.dot(_cover_matrix(n_cmp, n_slc, True), pg, precision=HI, preferred_element_type=F32)
    forced = (rowc == 0) | (rowc == own) | (rowc == own - 1)
    score = jnp.where(forced, 1e30, jnp.where(rowc > own, -1e30, imp))
    score = jnp.where(rowc < n_slc, score, -2e30)
    selb = _rank_lt(score, rowc, 0, n_slc, SLC_TOPK)
    selh = jnp.dot(selb.astype(MXU_DTYPE), unfold, preferred_element_type=F32).astype(MXU_DTYPE)
    blk_per_page = PAGE // SLC_BLOCK
    krow = lax.broadcasted_iota(jnp.int32, (PAGE, LANES), 0) >> _log2(SLC_BLOCK)
    kcol = lax.broadcasted_iota(jnp.int32, (PAGE, LANES), 1)
    parts = []
    for p in range(n_pages):
        expand = (kcol == krow + p * blk_per_page).astype(MXU_DTYPE)
        keep = jnp.dot(expand, selh, preferred_element_type=F32) > 0.5
        x = _dec_logits(cb[p][0, :, 2 * wb:3 * wb], qb, eb) * SCALE + tabs_ref[p * PAGE:(p + 1) * PAGE, :]
        parts.append((jnp.where(keep, x, NEG), cb[p][0, :, 3 * wb:4 * wb]))
    o_sel = attend(parts, z[:, B_KS:B_KS + wb], z[:, B_VS:B_VS + wb], self_ref[1:2, :])

    xw = _dec_logits(win_ref[0, :, 0:wb], qb, eb) * SCALE + tabw_ref[...]
    o_win = attend([(xw, win_ref[0, :, wb:2 * wb])], z[:, B_KW:B_KW + wb], z[:, B_VW:B_VW + wb], self_ref[1:2, :])

    sg = jax.nn.sigmoid(z[:, B_GATE:B_GATE + LANES])
    lane1 = lax.broadcasted_iota(jnp.int32, (1, LANES), 1)
    o_b = []
    for g in range(gb):
        for r in range(rb):
            base = (g * rb + r) * 3
            hs = slice(g * HEAD_DIM, (g + 1) * HEAD_DIM)
            mix = jnp.zeros((1, HEAD_DIM), F32)
            for c, o in enumerate((o_cmp, o_sel, o_win)):
                gc = jnp.sum(jnp.where(lane1 == base + c, sg, 0.0), axis=1, keepdims=True)
                mix = mix + gc * o[r][:, hs]
            o_b.append(mix)
    o_ref[0] = jnp.concatenate([o_a] + o_b, axis=1)


def _ab_sample(zs, cache_a, cache_b, win_buf, page_table, pe_flat, w1, w2, tabs):
    db = zs.shape[0]
    n_pages = page_table.shape[1]
    n_pool = cache_a.shape[0]
    ca = cache_a.reshape(n_pool, PAGE, 2 * H_KV_A * HEAD_DIM)
    cb = cache_b.reshape(n_pool, PAGE, 4 * G_B * HEAD_DIM)
    win = win_buf.reshape(db, win_buf.shape[1], 2 * G_B * HEAD_DIM)
    z3 = zs.reshape(db, 1, zs.shape[-1])
    tab_a, tab_s, tab_w, tab_self = tabs

    def page_spec(width, j):
        return pl.BlockSpec((1, PAGE, width), lambda b, pt: (pt[b, j], 0, 0))

    def full(a):
        return pl.BlockSpec(a.shape, lambda b, pt: (0,) * a.ndim)

    grid_spec = pltpu.PrefetchScalarGridSpec(
        num_scalar_prefetch=1,
        grid=(db,),
        in_specs=([pl.BlockSpec((1, 1, z3.shape[-1]), lambda b, pt: (b, 0, 0))]
                  + [page_spec(ca.shape[-1], j) for j in range(n_pages)]
                  + [page_spec(cb.shape[-1], j) for j in range(n_pages)]
                  + [pl.BlockSpec((1,) + win.shape[1:], lambda b, pt: (b, 0, 0)),
                     full(pe_flat), full(w1), full(w2), full(tab_a), full(tab_s), full(tab_w), full(tab_self)]),
        out_specs=pl.BlockSpec((1, 1, (H_A + H_B) * HEAD_DIM), lambda b, pt: (b, 0, 0)),
        scratch_shapes=[pltpu.VMEM((2 * G_B, n_pages * PAGE, HEAD_DIM), F32)],
    )
    out = pl.pallas_call(
        functools.partial(_ab_sample_kernel, n_pages=n_pages),
        out_shape=jax.ShapeDtypeStruct((db, 1, (H_A + H_B) * HEAD_DIM), F32),
        grid_spec=grid_spec,
        compiler_params=_cparams(("arbitrary",)),
        name="ab_sample",
    )(page_table, z3, *([ca] * n_pages), *([cb] * n_pages), win, pe_flat, w1, w2, tab_a, tab_s, tab_w, tab_self)
    return out.reshape(db, (H_A + H_B) * HEAD_DIM)


def _sample_tables_ab(t5_bias, past, wbuf):
    def lanes(t5_heads, ng, rep, dist):
        tab = t5_heads[_bucket_np(dist)]
        n = tab.shape[0]
        tab = tab.reshape(n, ng, rep).transpose(0, 2, 1).reshape(n, ng * rep)
        return jnp.pad(tab, ((0, 0), (0, LANES - ng * rep)))
    t5a, t5b = t5_bias[:, :H_A], t5_bias[:, H_A:H_A + H_B]
    tab_a = lanes(t5a, H_KV_A, REP_A, past - np.arange(past))
    tab_s = lanes(t5b, G_B, REP_B, past - np.arange(past))
    tab_w = lanes(t5b, G_B, REP_B, wbuf - np.arange(wbuf))
    self_a = lanes(t5a, H_KV_A, REP_A, np.zeros(1, np.int64))
    self_b = lanes(t5b, G_B, REP_B, np.zeros(1, np.int64))
    tab_self = jnp.pad(jnp.concatenate([self_a, self_b], axis=0), ((0, 6), (0, 0)))
    return tab_a, tab_s, tab_w, tab_self


def _dil_sample_kernel(z_ref, b1_ref, b2_ref, b3_ref, tab_ref, self_ref, o_ref):
    z = z_ref[0]
    hw = H_C * HEAD_DIM
    q = [z[:, 0:hw]]
    e = [_head_lanes(H_C, 1, 0, False)]
    et = [_head_lanes(H_C, 1, 0, True)]
    n_pat = len(C_PATTERNS)
    lself = (_dec_logits(_row8(z[:, hw:2 * hw]), q, e) * SCALE)[0:1] + self_ref[0:1, :]
    bufs = (b1_ref, b2_ref, b3_ref)
    lgs = [_dec_logits(b[0, :, 0:hw], q, e) * SCALE + tab_ref[i] for i, b in enumerate(bufs)]
    m = lself
    for x in lgs:
        m = jnp.maximum(m, jnp.max(x, axis=0, keepdims=True))
    es = jnp.exp(lself - m) * float(n_pat)
    den = es
    acc = jnp.dot(_row8(es).astype(MXU_DTYPE), et[0], preferred_element_type=F32)[0:1] * z[:, 2 * hw:3 * hw]
    for x, b in zip(lgs, bufs):
        ee = jnp.exp(x - m)
        den = den + jnp.sum(ee, axis=0, keepdims=True)
        acc = acc + _dec_pv(ee, b[0, :, hw:2 * hw], et)[0]
    inv = jnp.dot(_row8(1.0 / den), et[0].astype(F32), precision=HI, preferred_element_type=F32)[0:1]
    o_ref[0] = acc * inv


def _dil_sample(zs, buf, tab, tab_self):
    db = zs.shape[0]
    wc = buf.shape[1]
    hw = H_C * HEAD_DIM
    row = 2 * hw
    z3 = zs.reshape(db, 1, 3 * hw)
    specs, views = [], []
    for win, dil in C_PATTERNS:
        nrows = wc // dil
        views.append(buf.reshape(db, nrows, dil * row))
        specs.append(pl.BlockSpec((1, BAND, row), lambda b, blk=nrows // BAND - 1: (b, blk, 0)))
    out = pl.pallas_call(
        _dil_sample_kernel,
        out_shape=jax.ShapeDtypeStruct((db, 1, hw), F32),
        grid=(db,),
        in_specs=[pl.BlockSpec((1, 1, 3 * hw), lambda b: (b, 0, 0))] + specs
                 + [pl.BlockSpec(tab.shape, lambda b: (0, 0, 0)), pl.BlockSpec(tab_self.shape, lambda b: (0, 0))],
        out_specs=pl.BlockSpec((1, 1, hw), lambda b: (b, 0, 0)),
        compiler_params=_cparams(("parallel",)),
        name="dilated_sample",
    )(z3, *views, tab, tab_self)
    return out.reshape(db, hw)


def _sample_tables_c(t5_bias, wc):
    t5c = t5_bias[:, :H_C]
    tabs = []
    for win, dil in C_PATTERNS:
        dist = dil * (BAND - np.arange(BAND))
        tabs.append(jnp.pad(t5c[_bucket_np(dist)], ((0, 0), (0, LANES - H_C))))
    tab_self = jnp.pad(t5c[0:1], ((0, 7), (0, LANES - H_C)))
    return jnp.stack(tabs), tab_self


def _pad_cols(w, to):
    return jnp.pad(w, ((0, 0), (0, to - w.shape[1])))


def kernel(x_prompt, x_sample, cache_a_kv, cache_b_kv, state_b_win, state_c_kv, page_table, t5_bias, norm_g,
           w_ffn_in, w_ffn_out, w_in_ab, w_out_ab, cmp_pe, w_cmp_1, w_cmp_2, w_in_c, w_out_c):
    b, s, d = x_prompt.shape
    db = x_sample.shape[0]
    depth = norm_g.shape[0]
    d_ff = w_ffn_out.shape[2]
    ffp = -(-d_ff // 512) * 512
    past = page_table.shape[1] * cache_a_kv.shape[2]
    assert x_sample.shape[1] == 1 and s % TQ == 0 and past % MOBA_BLOCK == 0
    assert state_c_kv.shape[2] == C_PATTERNS[-1][0] and state_b_win.shape[2] == WIN_B

    xp = x_prompt.reshape(b * s, d)
    xs = x_sample.reshape(db, d)
    tm_p, tm_s = 512, db

    def ffn_weights(layer, half):
        w_in = w_ffn_in[layer, half]
        w_in_p = jnp.concatenate([_pad_cols(w_in[:, :d_ff], ffp), _pad_cols(w_in[:, d_ff:], ffp)], axis=1)
        w_out_p = jnp.pad(w_ffn_out[layer, half], ((0, ffp - d_ff), (0, 0)))
        return w_in_p.astype(MXU_DTYPE), w_out_p.astype(MXU_DTYPE)

    outs = {}
    for layer in range(depth):
        i = layer // 2
        g = norm_g[layer]
        w_in_p, w_out_p = ffn_weights(layer, 0)
        xp = _ffn_half(xp, g[0], g[1], w_in_p, w_out_p, tm_p)
        xs = _ffn_half(xs, g[0], g[1], w_in_p, w_out_p, tm_s)
        if layer % 2 == 0:
            w_in = _pad_cols(w_in_ab[i], P_AB_PAD).astype(MXU_DTYPE)
            zp = _norm_matmul(xp, g[2], w_in, tm_p)
            zs = _norm_matmul(xs, g[2], w_in, tm_s)
            zp3 = zp.reshape(b, s, P_AB_PAD)
            t5a, t5b = t5_bias[:, :H_A], t5_bias[:, H_A:H_A + H_B]
            tab_moba = _prompt_tables(t5a, H_KV_A, REP_A, [(0, None, False), (TQ, None, False), (0, None, True)])
            tab_sel = _prompt_tables(t5b, G_B, REP_B, [(0, None, False), (TQ, None, False), (0, None, True)])
            tab_win = _prompt_tables(t5b, G_B, REP_B, [(0, WIN_B, False), (TQ, WIN_B, False), (2 * TQ, WIN_B, False)])
            pe_flat = cmp_pe[i].reshape(2, 1, CMP_LEN * HEAD_DIM)
            w1 = w_cmp_1[i].astype(MXU_DTYPE)
            w2 = w_cmp_2[i].astype(MXU_DTYPE)
            o_a = _moba_prompt(zp3, tab_moba).reshape(b * s, H_A * HEAD_DIM)
            o_b = _nsa_prompt(zp3, pe_flat, w1, w2, tab_sel, tab_win).reshape(b * s, H_B * HEAD_DIM)
            w_out = w_out_ab[i].astype(MXU_DTYPE)
            xp = _out_proj(xp, o_a, o_b, 0, w_out, g[3], 256)
            o_s = _ab_sample(zs, cache_a_kv[i], cache_b_kv[i], state_b_win[i], page_table, pe_flat, w1, w2,
                             _sample_tables_ab(t5_bias, past, state_b_win.shape[2]))
            xs = _out_proj(xs, o_s, o_s, 1, w_out, g[3], db)
            outs.setdefault("a_p", []).append(zp3[:, :, A_K:B_Q].reshape(b, s, 2, H_KV_A, HEAD_DIM))
            outs.setdefault("a_s", []).append(zs[:, A_K:B_Q].reshape(db, 1, 2, H_KV_A, HEAD_DIM))
            outs.setdefault("b_p", []).append(zp3[:, :, B_KC:B_KW].reshape(b, s, 4, G_B, HEAD_DIM))
            outs.setdefault("b_s", []).append(zs[:, B_KC:B_KW].reshape(db, 1, 4, G_B, HEAD_DIM))
            nw = min(WIN_B, s)
            outs.setdefault("w_p", []).append(zp3[:, s - nw:, B_KW:B_GATE].reshape(b, nw, 2, G_B, HEAD_DIM))
            outs.setdefault("w_s", []).append(zs[:, B_KW:B_GATE].reshape(db, 1, 2, G_B, HEAD_DIM))
        else:
            w_in = w_in_c[i].astype(MXU_DTYPE)
            zp = _norm_matmul(xp, g[2], w_in, tm_p)
            zs = _norm_matmul(xs, g[2], w_in, tm_s)
            zp3 = zp.reshape(b, s, P_C)
            t5c = t5_bias[:, :H_C]
            parts = [_dil_prompt(zp3, tab, dil) for tab, (_, dil) in zip(_dil_tables(t5c), C_PATTERNS)]
            o_c = _dil_combine(parts)
            w_out = w_out_c[i].astype(MXU_DTYPE)
            xp = _out_proj(xp, o_c, o_c, 1, w_out, g[3], 256)
            tab_c, self_c = _sample_tables_c(t5_bias, state_c_kv.shape[2])
            o_s = _dil_sample(zs, state_c_kv[i], tab_c, self_c)
            xs = _out_proj(xs, o_s, o_s, 1, w_out, g[3], db)
            hw = H_C * HEAD_DIM
            nc = min(C_PATTERNS[-1][0], s)
            outs.setdefault("c_p", []).append(zp3[:, s - nc:, hw:].reshape(b, nc, 2, H_C, HEAD_DIM))
            outs.setdefault("c_s", []).append(zs[:, hw:].reshape(db, 1, 2, H_C, HEAD_DIM))
        w_in_p, w_out_p = ffn_weights(layer, 1)
        xp = _ffn_half(xp, g[4], g[5], w_in_p, w_out_p, tm_p)
        xs = _ffn_half(xs, g[4], g[5], w_in_p, w_out_p, tm_s)
    return (xp.reshape(b, s, d), xs.reshape(db, 1, d)) + tuple(
        jnp.stack(outs[k]) for k in ("a_p", "a_s", "b_p", "b_s", "w_p", "w_s", "c_p", "c_s"))
```

```python
import functools
import math

import numpy as np
import jax
import jax.numpy as jnp
from jax import lax
from jax.experimental import pallas as pl
from jax.experimental.pallas import tpu as pltpu

F32 = jnp.float32
MXU_DTYPE = jnp.bfloat16
HI = lax.Precision.HIGHEST
NEG = -1e30
TINY = 1e-30
EPS = 1e-6
LANES = 128
SUBLANES = 8
VMEM_LIMIT = 56 * 1024 * 1024

HEAD_DIM = 128
SCALE = HEAD_DIM ** -0.5
N_BUCKETS = 32
T5_MAX_DIST = 128
H_A, H_KV_A, H_B, G_B, H_C = 8, 4, 8, 2, 16
REP_A = H_A // H_KV_A
REP_B = H_B // G_B
MOBA_BLOCK, MOBA_TOPK = 256, 3
CMP_LEN, CMP_STRIDE = 32, 16
SLC_BLOCK, SLC_TOPK = 64, 16
WIN_B = 512
C_PATTERNS = ((128, 1), (512, 4), (2048, 16))
PAGE = 128
TQ = 256
BAND = 128
NO_LIMIT = 1 << 30

A_Q, A_K, A_V, B_Q = 0, 1024, 1536, 2048
B_KC, B_VC, B_KS, B_VS, B_KW, B_VW, B_GATE = 3072, 3328, 3584, 3840, 4096, 4352, 4608
P_AB = 4632
P_AB_PAD = 5120
P_C = 6144

NT = (((1,), (1,)), ((), ()))


def _cparams(sem):
    return pltpu.CompilerParams(dimension_semantics=sem, vmem_limit_bytes=VMEM_LIMIT)


def _log2(n):
    assert n & (n - 1) == 0
    return n.bit_length() - 1


def _rms(x, g):
    return x * lax.rsqrt(jnp.mean(x * x, axis=-1, keepdims=True) + EPS) * g


def _ffn_kernel(x_ref, gpre_ref, gpost_ref, wg_ref, wu_ref, wo_ref, o_ref, h_sc):
    j = pl.program_id(1)

    @pl.when(j == 0)
    def _():
        h_sc[...] = _rms(x_ref[...], gpre_ref[...]).astype(h_sc.dtype)
        o_ref[...] = jnp.zeros_like(o_ref)

    h = h_sc[...]
    g = jnp.dot(h, wg_ref[...], preferred_element_type=F32)
    u = jnp.dot(h, wu_ref[...], preferred_element_type=F32)
    a = g * jax.nn.sigmoid(g) * u
    o_ref[...] += jnp.dot(a.astype(MXU_DTYPE), wo_ref[...], preferred_element_type=F32)

    @pl.when(j == pl.num_programs(1) - 1)
    def _():
        o_ref[...] = x_ref[...] + 0.5 * _rms(o_ref[...], gpost_ref[...])


def _ffn_half(x, g_pre, g_post, w_in_p, w_out_p, tm, tf=512):
    n, d = x.shape
    ffp = w_out_p.shape[0]
    nff = ffp // tf
    return pl.pallas_call(
        _ffn_kernel,
        out_shape=jax.ShapeDtypeStruct((n, d), F32),
        grid=(n // tm, nff),
        in_specs=[
            pl.BlockSpec((tm, d), lambda i, j: (i, 0), pipeline_mode=pl.Buffered(1)),
            pl.BlockSpec((1, d), lambda i, j: (0, 0)),
            pl.BlockSpec((1, d), lambda i, j: (0, 0)),
            pl.BlockSpec((d, tf), lambda i, j: (0, j)),
            pl.BlockSpec((d, tf), lambda i, j: (0, j + nff)),
            pl.BlockSpec((tf, d), lambda i, j: (j, 0)),
        ],
        out_specs=pl.BlockSpec((tm, d), lambda i, j: (i, 0)),
        scratch_shapes=[pltpu.VMEM((tm, d), MXU_DTYPE)],
        compiler_params=_cparams(("parallel", "arbitrary")),
        name="ffn_half",
    )(x, g_pre.reshape(1, d), g_post.reshape(1, d), w_in_p, w_in_p, w_out_p)


def _norm_mm_kernel(x_ref, g_ref, w_ref, o_ref, h_sc):
    @pl.when(pl.program_id(1) == 0)
    def _():
        h_sc[...] = _rms(x_ref[...], g_ref[...]).astype(h_sc.dtype)

    o_ref[...] = jnp.dot(h_sc[...], w_ref[...], preferred_element_type=F32)


def _norm_matmul(x, g, w, tm, tn=1024):
    n, d = x.shape
    p = w.shape[1]
    return pl.pallas_call(
        _norm_mm_kernel,
        out_shape=jax.ShapeDtypeStruct((n, p), F32),
        grid=(n // tm, p // tn),
        in_specs=[
            pl.BlockSpec((tm, d), lambda i, j: (i, 0)),
            pl.BlockSpec((1, d), lambda i, j: (0, 0)),
            pl.BlockSpec((d, tn), lambda i, j: (0, j)),
        ],
        out_specs=pl.BlockSpec((tm, tn), lambda i, j: (i, j)),
        scratch_shapes=[pltpu.VMEM((tm, d), MXU_DTYPE)],
        compiler_params=_cparams(("parallel", "arbitrary")),
        name="norm_matmul",
    )(x, g.reshape(1, d), w)


def _mm_norm_res_kernel(x_ref, o1_ref, o2_ref, w1_ref, w2_ref, g_ref, out_ref):
    y = jnp.dot(o1_ref[...].astype(MXU_DTYPE), w1_ref[...], preferred_element_type=F32)
    y = y + jnp.dot(o2_ref[...].astype(MXU_DTYPE), w2_ref[...], preferred_element_type=F32)
    out_ref[...] = x_ref[...] + _rms(y, g_ref[...])


def _out_proj(x, o1, o2, o2_col, w, g, tm):
    n, d = x.shape
    half = d // 2
    return pl.pallas_call(
        _mm_norm_res_kernel,
        out_shape=jax.ShapeDtypeStruct((n, d), F32),
        grid=(n // tm,),
        in_specs=[
            pl.BlockSpec((tm, d), lambda i: (i, 0)),
            pl.BlockSpec((tm, half), lambda i: (i, 0)),
            pl.BlockSpec((tm, half), lambda i: (i, o2_col)),
            pl.BlockSpec((half, d), lambda i: (0, 0)),
            pl.BlockSpec((half, d), lambda i: (1, 0)),
            pl.BlockSpec((1, d), lambda i: (0, 0)),
        ],
        out_specs=pl.BlockSpec((tm, d), lambda i: (i, 0)),
        compiler_params=_cparams(("parallel",)),
        name="out_proj",
    )(x, o1, o2, w, w, g.reshape(1, d))


def _bucket_vec(dist):
    exact = N_BUCKETS // 2
    d = jnp.maximum(dist, 1).astype(F32)
    far = exact + (jnp.log(d / exact) / math.log(T5_MAX_DIST / exact) * (N_BUCKETS - exact)).astype(jnp.int32)
    return jnp.where(dist < exact, dist, jnp.minimum(far, N_BUCKETS - 1))


def _toeplitz_kernel(par_ref, t5_ref, o_ref, *, rep, head0):
    kind = pl.program_id(1)
    h = head0 + pl.program_id(0) * rep + pl.program_id(2)
    off, max_back, scale, const = par_ref[kind, 0], par_ref[kind, 1], par_ref[kind, 2], par_ref[kind, 3]
    shape = o_ref.shape[2:]
    dist = off + lax.broadcasted_iota(jnp.int32, shape, 0) - lax.broadcasted_iota(jnp.int32, shape, 1)
    valid = ((dist >= 0) & (dist <= max_back)) | (const > 0)
    bkt = _bucket_vec(jnp.where(const > 0, T5_MAX_DIST, jnp.maximum(dist, 0) * scale))
    acc = jnp.zeros(shape, F32)
    for k in range(N_BUCKETS):
        acc = jnp.where(bkt == k, t5_ref[k, h], acc)
    o_ref[0, 0] = jnp.where(valid, acc, NEG)


def _toeplitz_tables(t5_bias, head0, n_groups, rep, rows, cols, kinds):
    par = jnp.asarray(np.asarray(kinds, np.int32))
    smem = pl.BlockSpec(memory_space=pltpu.SMEM)
    return pl.pallas_call(
        functools.partial(_toeplitz_kernel, rep=rep, head0=head0),
        out_shape=jax.ShapeDtypeStruct((n_groups, len(kinds), rep * rows, cols), F32),
        grid=(n_groups, len(kinds), rep),
        in_specs=[smem, smem],
        out_specs=pl.BlockSpec((1, 1, rows, cols), lambda g, k, r: (g, k, r, 0)),
        compiler_params=_cparams(("parallel", "parallel", "parallel")),
        name="bias_toeplitz",
    )(par, t5_bias)


def _row_table_kernel(t5p_ref, o_ref, *, base, step, shift, pick):
    shape = o_ref.shape
    row = lax.broadcasted_iota(jnp.int32, shape, 0)
    bkt = _bucket_vec(base - step * (row >> shift))
    acc = jnp.zeros(shape, F32)
    for k in range(N_BUCKETS):
        acc = jnp.where(bkt == k, t5p_ref[k:k + 1, :], acc)
    if pick:
        lane = lax.broadcasted_iota(jnp.int32, shape, 1)
        one = jnp.sum(jnp.where(lane == (row & ((1 << shift) - 1)), acc, 0.0), axis=1, keepdims=True)
        acc = jnp.broadcast_to(one, shape)
    o_ref[...] = acc


def _row_table(t5p, n, base, step, shift=0, pick=False):
    return pl.pallas_call(
        functools.partial(_row_table_kernel, base=base, step=step, shift=shift, pick=pick),
        out_shape=jax.ShapeDtypeStruct((n, LANES), F32),
        name="bias_rows",
    )(t5p)


def _head_lane_table(t5_heads, ng, rep):
    t = t5_heads.reshape(N_BUCKETS, ng, rep).transpose(0, 2, 1).reshape(N_BUCKETS, ng * rep)
    return jnp.pad(t, ((0, 0), (0, LANES - ng * rep)))


def _stack_heads(q, n):
    return jnp.concatenate([q[:, r * HEAD_DIM:(r + 1) * HEAD_DIM] for r in range(n)], axis=0)


def _unstack_heads(o, n):
    t = o.shape[0] // n
    return jnp.concatenate([o[r * t:(r + 1) * t] for r in range(n)], axis=1)


def _rank_lt(score, idx, axis, n_items, k):
    sel = jnp.zeros(score.shape, F32)
    for n in range(n_items):
        sn = score[:, n:n + 1] if axis == 1 else score[n:n + 1, :]
        beats = (score > sn) | ((score == sn) & (idx < n))
        rank = jnp.sum(beats.astype(F32), axis=axis, keepdims=True)
        sel = jnp.where((idx == n) & (rank < k), 1.0, sel)
    return sel


def _rows_to_cols(x_t):
    n, t = x_t.shape
    return jnp.concatenate([x_t, jnp.zeros((LANES - n, t), x_t.dtype)], axis=0).T


def _two_pass_attention(n_tiles, logits_fn, v_fn, s_sc, mx_sc, ls_sc, acc_sc):
    mx_sc[...] = jnp.full(mx_sc.shape, NEG, F32)

    def pass1(t, c):
        s = logits_fn(t)
        s_sc[t] = s
        mx_sc[...] = jnp.maximum(mx_sc[...], s)
        return c

    lax.fori_loop(0, n_tiles, pass1, 0)
    m = jnp.broadcast_to(jnp.max(mx_sc[...], axis=-1, keepdims=True), mx_sc.shape)
    ls_sc[...] = jnp.zeros_like(ls_sc)
    acc_sc[...] = jnp.zeros_like(acc_sc)

    def pass2(t, c):
        p = jnp.exp(s_sc[t] - m)
        ls_sc[...] += p
        acc_sc[...] += jnp.dot(p.astype(MXU_DTYPE), v_fn(t), preferred_element_type=F32)
        return c

    lax.fori_loop(0, n_tiles, pass2, 0)
    return acc_sc[...] / jnp.sum(ls_sc[...], axis=-1, keepdims=True)


def _moba_prompt_kernel(q_ref, k_ref, v_ref, tab_ref, o_ref, kmean_sc, s_sc, mx_sc, ls_sc, acc_sc, *, nblk):
    qi = pl.program_id(2)

    @pl.when(qi == 0)
    def _():
        kmean_sc[...] = jnp.zeros_like(kmean_sc)
        for n in range(nblk):
            blk = k_ref[0, n * MOBA_BLOCK:(n + 1) * MOBA_BLOCK, :]
            kmean_sc[n:n + 1, :] = jnp.sum(blk, axis=0, keepdims=True) * (1.0 / MOBA_BLOCK)

    qs = _stack_heads(q_ref[0], REP_A)
    nb8 = -(-nblk // SUBLANES) * SUBLANES
    gate_t = lax.dot_general(kmean_sc[...], qs, NT, precision=HI, preferred_element_type=F32)[0:nb8]
    blk_id = lax.broadcasted_iota(jnp.int32, gate_t.shape, 0)
    gate_t = jnp.where(blk_id < qi, gate_t, -jnp.inf)
    sel_t = _rank_lt(gate_t, blk_id, 0, nblk, MOBA_TOPK)
    sel_t = jnp.where(blk_id < qi, sel_t, 0.0)
    sel_t = jnp.where(blk_id == qi, 1.0, sel_t)
    sel = _rows_to_cols(sel_t).astype(MXU_DTYPE)

    qb = qs.astype(MXU_DTYPE)
    pick_row = lax.broadcasted_iota(jnp.int32, (LANES, MOBA_BLOCK), 0)

    def logits(j):
        start = pl.multiple_of(j * MOBA_BLOCK, MOBA_BLOCK)
        k = k_ref[0, pl.ds(start, MOBA_BLOCK), :].astype(MXU_DTYPE)
        s = lax.dot_general(qb, k, NT, preferred_element_type=F32) * SCALE + tab_ref[0, jnp.minimum(qi - j, 2)]
        keep = jnp.dot(sel, (pick_row == j).astype(MXU_DTYPE), preferred_element_type=F32)
        return jnp.where(keep > 0.5, s, NEG)

    def values(j):
        start = pl.multiple_of(j * MOBA_BLOCK, MOBA_BLOCK)
        return v_ref[0, pl.ds(start, MOBA_BLOCK), :].astype(MXU_DTYPE)

    out = _two_pass_attention(qi + 1, logits, values, s_sc, mx_sc, ls_sc, acc_sc)
    o_ref[0] = _unstack_heads(out, REP_A)


def _moba_prompt(z3, tab):
    b, s, _ = z3.shape
    nq = s // TQ
    rows = REP_A * TQ
    return pl.pallas_call(
        functools.partial(_moba_prompt_kernel, nblk=s // MOBA_BLOCK),
        out_shape=jax.ShapeDtypeStruct((b, s, H_A * HEAD_DIM), F32),
        grid=(b, H_KV_A, nq),
        in_specs=[
            pl.BlockSpec((1, TQ, REP_A * HEAD_DIM), lambda bi, h, qi: (bi, qi, h)),
            pl.BlockSpec((1, s, HEAD_DIM), lambda bi, h, qi: (bi, 0, A_K // HEAD_DIM + h)),
            pl.BlockSpec((1, s, HEAD_DIM), lambda bi, h, qi: (bi, 0, A_V // HEAD_DIM + h)),
            pl.BlockSpec((1, 3, rows, TQ), lambda bi, h, qi: (h, 0, 0, 0)),
        ],
        out_specs=pl.BlockSpec((1, TQ, REP_A * HEAD_DIM), lambda bi, h, qi: (bi, qi, h)),
        scratch_shapes=[pltpu.VMEM((LANES, HEAD_DIM), F32), pltpu.VMEM((nq, rows, TQ), F32),
                        pltpu.VMEM((rows, TQ), F32), pltpu.VMEM((rows, TQ), F32),
                        pltpu.VMEM((rows, HEAD_DIM), F32)],
        compiler_params=_cparams(("parallel", "parallel", "arbitrary")),
        name="moba_prompt",
    )(z3, z3, z3, tab)


def _compress_seq(x_refs, w1_ref, w2_ref, pe_ref, slot):
    n_rows = 128
    ng = len(x_refs)
    acc_a = jnp.zeros((ng * n_rows, HEAD_DIM), F32)
    acc_b = jnp.zeros((ng * n_rows, HEAD_DIM), F32)
    for u in range(CMP_STRIDE):
        xs = jnp.concatenate([x[pl.ds(u, n_rows, stride=CMP_STRIDE), :] for x in x_refs], axis=0)
        xs = xs.astype(MXU_DTYPE)
        acc_a += jnp.dot(xs, w1_ref[slot, u * HEAD_DIM:(u + 1) * HEAD_DIM, :], preferred_element_type=F32)
        acc_b += jnp.dot(xs, w1_ref[slot, (CMP_STRIDE + u) * HEAD_DIM:(CMP_STRIDE + u + 1) * HEAD_DIM, :],
                         preferred_element_type=F32)
    pe8 = jnp.broadcast_to(pe_ref[slot], (SUBLANES, CMP_LEN * HEAD_DIM)).astype(MXU_DTYPE)
    c = jnp.dot(pe8, w1_ref[slot], preferred_element_type=F32)[0:1]
    outs = []
    for g in range(ng):
        a = acc_a[g * n_rows:(g + 1) * n_rows]
        bsh = pltpu.roll(acc_b[g * n_rows:(g + 1) * n_rows], n_rows - 1, 0)
        y = a + bsh + c
        hid = (y * jax.nn.sigmoid(y)).astype(MXU_DTYPE)
        outs.append(jnp.dot(hid, w2_ref[slot], preferred_element_type=F32))
    return jnp.concatenate(outs, axis=1) if ng > 1 else outs[0]


def _cover_matrix(n_cmp, n_slc, transposed):
    shape = (LANES, LANES)
    n = lax.broadcasted_iota(jnp.int32, shape, 1 if transposed else 0)
    s = lax.broadcasted_iota(jnp.int32, shape, 0 if transposed else 1)
    c = ((n * CMP_STRIDE < (s + 1) * SLC_BLOCK) & (n * CMP_STRIDE + CMP_LEN > s * SLC_BLOCK)
         & (n < n_cmp) & (s < n_slc))
    return c.astype(F32)


def _slc_scores(imp_t, pos, n_slc):
    blk = lax.broadcasted_iota(jnp.int32, imp_t.shape, 0)
    own = pos >> _log2(SLC_BLOCK)
    forced = (blk == 0) | (blk == own) | (blk == own - 1)
    score = jnp.where(forced, 1e30, jnp.where(blk > own, -1e30, imp_t))
    return jnp.where(blk < n_slc, score, -2e30), blk


def _nsa_prompt_kernel(q_ref, kc_ref, vc_ref, ks_ref, vs_ref, kw_ref, vw_ref, gate_ref, pe_ref, w1_ref,
                       w2_ref, tabs_ref, tabw_ref, o_ref, kcmp_sc, vcmp_sc, s_sc, mx_sc, ls_sc, acc_sc,
                       *, n_cmp, n_slc):
    g = pl.program_id(1)
    qi = pl.program_id(2)

    @pl.when(qi == 0)
    def _():
        kcmp_sc[...] = _compress_seq([kc_ref.at[0]], w1_ref, w2_ref, pe_ref, 0)
        vcmp_sc[...] = _compress_seq([vc_ref.at[0]], w1_ref, w2_ref, pe_ref, 1)

    rows = REP_B * TQ
    qs = _stack_heads(q_ref[0], REP_B)
    qb = qs.astype(MXU_DTYPE)

    s = lax.dot_general(qb, kcmp_sc[...].astype(MXU_DTYPE), NT, preferred_element_type=F32) * SCALE
    t_rows = qi * TQ + (lax.broadcasted_iota(jnp.int32, (rows, LANES), 0) & (TQ - 1))
    lane_rows = lax.broadcasted_iota(jnp.int32, (rows, LANES), 1)
    ok = (lane_rows * CMP_STRIDE + CMP_LEN - 1 <= t_rows) & (lane_rows < n_cmp)
    s = jnp.where(ok, s, NEG)
    m = jnp.max(s, axis=-1, keepdims=True)
    e = jnp.where(ok, jnp.exp(s - m), 0.0)
    p = e / jnp.maximum(jnp.sum(e, axis=-1, keepdims=True), TINY)
    o_cmp = jnp.dot(p.astype(MXU_DTYPE), vcmp_sc[...].astype(MXU_DTYPE), preferred_element_type=F32)

    psum = p[0:TQ]
    for r in range(1, REP_B):
        psum = psum + p[r * TQ:(r + 1) * TQ]
    imp_t = lax.dot_general(_cover_matrix(n_cmp, n_slc, True), psum, NT, precision=HI,
                            preferred_element_type=F32)
    ns8 = -(-n_slc // SUBLANES) * SUBLANES
    pos = qi * TQ + lax.broadcasted_iota(jnp.int32, (ns8, TQ), 1)
    score, blk = _slc_scores(imp_t[0:ns8], pos, n_slc)
    sel = _rows_to_cols(_rank_lt(score, blk, 0, n_slc, SLC_TOPK)).astype(MXU_DTYPE)

    blk_of_lane = lax.broadcasted_iota(jnp.int32, (LANES, TQ), 1) >> _log2(SLC_BLOCK)
    blk_row = lax.broadcasted_iota(jnp.int32, (LANES, TQ), 0)

    def sel_logits(j):
        start = pl.multiple_of(j * TQ, TQ)
        k = ks_ref[0, pl.ds(start, TQ), :].astype(MXU_DTYPE)
        expand = (blk_row == blk_of_lane + j * (TQ // SLC_BLOCK)).astype(MXU_DTYPE)
        keep1 = jnp.dot(sel, expand, preferred_element_type=F32)
        keep = jnp.concatenate([keep1] * REP_B, axis=0) > 0.5
        sc = lax.dot_general(qb, k, NT, preferred_element_type=F32) * SCALE + tabs_ref[0, jnp.minimum(qi - j, 2)]
        return jnp.where(keep, sc, NEG)

    def sel_values(j):
        return vs_ref[0, pl.ds(pl.multiple_of(j * TQ, TQ), TQ), :].astype(MXU_DTYPE)

    o_sel = _two_pass_attention(qi + 1, sel_logits, sel_values, s_sc, mx_sc, ls_sc, acc_sc)

    def win_logits(d):
        start = pl.multiple_of((qi - d) * TQ, TQ)
        k = kw_ref[0, pl.ds(start, TQ), :].astype(MXU_DTYPE)
        return lax.dot_general(qb, k, NT, preferred_element_type=F32) * SCALE + tabw_ref[0, d]

    def win_values(d):
        return vw_ref[0, pl.ds(pl.multiple_of((qi - d) * TQ, TQ), TQ), :].astype(MXU_DTYPE)

    o_win = _two_pass_attention(jnp.minimum(qi, WIN_B // TQ) + 1, win_logits, win_values,
                                s_sc, mx_sc, ls_sc, acc_sc)

    sg = jax.nn.sigmoid(gate_ref[0])
    lane = lax.broadcasted_iota(jnp.int32, (TQ, LANES), 1)
    outs = []
    for r in range(REP_B):
        base = (g * REP_B + r) * 3
        mix = jnp.zeros((TQ, HEAD_DIM), F32)
        for c, o in enumerate((o_cmp, o_sel, o_win)):
            gc = jnp.sum(jnp.where(lane == base + c, sg, 0.0), axis=1, keepdims=True)
            mix = mix + gc * o[r * TQ:(r + 1) * TQ]
        outs.append(mix)
    o_ref[0] = jnp.concatenate(outs, axis=1)


def _nsa_prompt(z3, pe_flat, w1, w2, tab_sel, tab_win):
    b, s, _ = z3.shape
    nq = s // TQ
    rows = REP_B * TQ
    n_cmp = (s - CMP_LEN) // CMP_STRIDE + 1
    n_slc = -(-s // SLC_BLOCK)

    def seq_spec(col):
        return pl.BlockSpec((1, s, HEAD_DIM), lambda bi, g, qi: (bi, 0, col // HEAD_DIM + g))

    def full(a):
        return pl.BlockSpec(a.shape, lambda bi, g, qi: (0,) * a.ndim)

    qw = REP_B * HEAD_DIM
    return pl.pallas_call(
        functools.partial(_nsa_prompt_kernel, n_cmp=n_cmp, n_slc=n_slc),
        out_shape=jax.ShapeDtypeStruct((b, s, H_B * HEAD_DIM), F32),
        grid=(b, G_B, nq),
        in_specs=[
            pl.BlockSpec((1, TQ, qw), lambda bi, g, qi: (bi, qi, B_Q // qw + g)),
            seq_spec(B_KC), seq_spec(B_VC), seq_spec(B_KS), seq_spec(B_VS), seq_spec(B_KW), seq_spec(B_VW),
            pl.BlockSpec((1, TQ, LANES), lambda bi, g, qi: (bi, qi, B_GATE // LANES)),
            full(pe_flat), full(w1), full(w2),
            pl.BlockSpec((1, 3, rows, TQ), lambda bi, g, qi: (g, 0, 0, 0)),
            pl.BlockSpec((1, 3, rows, TQ), lambda bi, g, qi: (g, 0, 0, 0)),
        ],
        out_specs=pl.BlockSpec((1, TQ, qw), lambda bi, g, qi: (bi, qi, g)),
        scratch_shapes=[pltpu.VMEM((LANES, HEAD_DIM), F32), pltpu.VMEM((LANES, HEAD_DIM), F32),
                        pltpu.VMEM((nq, rows, TQ), F32), pltpu.VMEM((rows, TQ), F32),
                        pltpu.VMEM((rows, TQ), F32), pltpu.VMEM((rows, HEAD_DIM), F32)],
        compiler_params=_cparams(("parallel", "parallel", "arbitrary")),
        name="nsa_prompt",
    )(z3, z3, z3, z3, z3, z3, z3, z3, pe_flat, w1, w2, tab_sel, tab_win)


def _dil_prompt_kernel(q_ref, kp_ref, kc_ref, vp_ref, vc_ref, tab_ref, o_ref, m_ref, l_ref, *, pat):
    qi = pl.program_id(2)
    col = lax.broadcasted_iota(jnp.int32, (BAND, 2 * BAND), 1)
    no_prev = (col < BAND) & (qi == 0)
    lane = lax.broadcasted_iota(jnp.int32, (BAND, LANES), 1)
    m_all = jnp.zeros((BAND, LANES), F32)
    l_all = jnp.zeros((BAND, LANES), F32)
    for h in range(H_C):
        hs = slice(h * HEAD_DIM, (h + 1) * HEAD_DIM)
        q = q_ref[0, :, hs].astype(MXU_DTYPE)
        k = jnp.concatenate([kp_ref[0, :, hs], kc_ref[0, :, hs]], axis=0).astype(MXU_DTYPE)
        v = jnp.concatenate([vp_ref[0, :, hs], vc_ref[0, :, hs]], axis=0).astype(MXU_DTYPE)
        s = lax.dot_general(q, k, NT, preferred_element_type=F32) * SCALE + tab_ref[h, pat]
        s = jnp.where(no_prev, NEG, s)
        m = jnp.max(s, axis=-1, keepdims=True)
        e = jnp.exp(s - m)
        l = jnp.sum(e, axis=-1, keepdims=True)
        p = e / jnp.maximum(l, TINY)
        o_ref[0, :, hs] = jnp.dot(p.astype(MXU_DTYPE), v, preferred_element_type=F32)
        m_all = jnp.where(lane == h, m, m_all)
        l_all = jnp.where(lane == h, l, l_all)
    m_ref[0] = m_all
    l_ref[0] = l_all


def _dil_prompt(zc3, tab, pat, dil):
    b, s, _ = zc3.shape
    sf = s // dil
    zv = zc3.reshape(b, sf, dil * P_C)
    nq = sf // BAND
    hw = H_C * HEAD_DIM

    def spec(part, prev):
        if prev:
            return pl.BlockSpec((1, BAND, hw), lambda bi, r, qi: (bi, jnp.maximum(qi - 1, 0), r * 3 + part))
        return pl.BlockSpec((1, BAND, hw), lambda bi, r, qi: (bi, qi, r * 3 + part))

    o, m, l = pl.pallas_call(
        functools.partial(_dil_prompt_kernel, pat=pat),
        out_shape=(jax.ShapeDtypeStruct((b, sf, dil * hw), F32),
                   jax.ShapeDtypeStruct((b, sf, dil * LANES), F32),
                   jax.ShapeDtypeStruct((b, sf, dil * LANES), F32)),
        grid=(b, dil, nq),
        in_specs=[spec(0, False), spec(1, True), spec(1, False), spec(2, True), spec(2, False),
                  pl.BlockSpec(tab.shape, lambda bi, r, qi: (0, 0, 0, 0))],
        out_specs=(pl.BlockSpec((1, BAND, hw), lambda bi, r, qi: (bi, qi, r)),
                   pl.BlockSpec((1, BAND, LANES), lambda bi, r, qi: (bi, qi, r)),
                   pl.BlockSpec((1, BAND, LANES), lambda bi, r, qi: (bi, qi, r))),
        compiler_params=_cparams(("parallel", "parallel", "arbitrary")),
        name=f"dilated_prompt_d{dil}",
    )(zv, zv, zv, zv, zv, tab)
    return o.reshape(b * s, hw), m.reshape(b * s, LANES), l.reshape(b * s, LANES)


def _dil_combine_kernel(o1, o2, o3, m1, m2, m3, l1, l2, l3, out_ref):
    ms = (m1[...], m2[...], m3[...])
    top = jnp.maximum(jnp.maximum(ms[0], ms[1]), ms[2])
    ws = [l[...] * jnp.exp(m - top) for l, m in zip((l1, l2, l3), ms)]
    den = ws[0] + ws[1] + ws[2]
    for h in range(H_C):
        hs = slice(h * HEAD_DIM, (h + 1) * HEAD_DIM)
        num = sum(w[:, h:h + 1] * o[:, hs] for w, o in zip(ws, (o1, o2, o3)))
        out_ref[:, hs] = num / den[:, h:h + 1]


def _dil_combine(parts, tm=256):
    (o1, m1, l1), (o2, m2, l2), (o3, m3, l3) = parts
    n, hw = o1.shape
    big = pl.BlockSpec((tm, hw), lambda i: (i, 0))
    small = pl.BlockSpec((tm, LANES), lambda i: (i, 0))
    return pl.pallas_call(
        _dil_combine_kernel,
        out_shape=jax.ShapeDtypeStruct((n, hw), F32),
        grid=(n // tm,),
        in_specs=[big] * 3 + [small] * 6,
        out_specs=big,
        compiler_params=_cparams(("parallel",)),
        name="dilated_combine",
    )(o1, o2, o3, m1, m2, m3, l1, l2, l3)


def _head_lanes(ng, rep, r, transposed):
    shape = (LANES, ng * HEAD_DIM) if transposed else (ng * HEAD_DIM, LANES)
    c = lax.broadcasted_iota(jnp.int32, shape, 1 if transposed else 0)
    lane = lax.broadcasted_iota(jnp.int32, shape, 0 if transposed else 1)
    return (lane == r * ng + (c >> _log2(HEAD_DIM))).astype(MXU_DTYPE)


def _q_groups(q, ng, rep):
    return [jnp.concatenate([q[:, (g * rep + r) * HEAD_DIM:(g * rep + r + 1) * HEAD_DIM] for g in range(ng)],
                            axis=1) for r in range(rep)]


def _dec_logits(k, q_rs, e_rs, precision=None):
    out = None
    for q_r, e_r in zip(q_rs, e_rs):
        prod = k * q_r
        if precision is None:
            t = jnp.dot(prod.astype(MXU_DTYPE), e_r, preferred_element_type=F32)
        else:
            t = jnp.dot(prod, e_r.astype(F32), precision=precision, preferred_element_type=F32)
        out = t if out is None else out + t
    return out


def _dec_pv(p, v, et_rs):
    pb = p.astype(MXU_DTYPE)
    return [jnp.sum(jnp.dot(pb, et, preferred_element_type=F32) * v, axis=0, keepdims=True) for et in et_rs]


def _row8(x):
    return jnp.broadcast_to(x, (SUBLANES, x.shape[1]))


def _heads_out(o_rs, ng, rep):
    return jnp.concatenate([o_rs[r][:, g * HEAD_DIM:(g + 1) * HEAD_DIM] for g in range(ng) for r in range(rep)],
                           axis=1)


def _head_rows(ref, first, n_heads, n_rows, rows_per_pos):
    return jnp.concatenate([ref[pl.ds(first + h, n_rows, stride=rows_per_pos), :] for h in range(n_heads)], axis=1)


def _ab_sample_kernel(pt_ref, z_ref, *refs, n_pages):
    ca = refs[:n_pages]
    cb = refs[n_pages:2 * n_pages]
    (win_ref, pe_ref, w1_ref, w2_ref, taba_ref, tabs_ref, tabw_ref, self_ref, o_ref, seq_sc) = refs[2 * n_pages:]
    del pt_ref
    b = pl.program_id(0)
    z = z_ref[pl.ds(b, 1), :]
    past = n_pages * PAGE

    ga, ra = H_KV_A, REP_A
    wa = ga * HEAD_DIM
    rpp_a = 2 * ga
    qa = _q_groups(z[:, A_Q:A_Q + H_A * HEAD_DIM], ga, ra)
    ea = [_head_lanes(ga, ra, r, False) for r in range(ra)]
    eta = [_head_lanes(ga, ra, r, True) for r in range(ra)]
    pages_per_blk = MOBA_BLOCK // PAGE
    nblk = past // MOBA_BLOCK
    lg, ksum = [], []
    for p in range(n_pages):
        kp = _head_rows(ca[p], 0, ga, PAGE, rpp_a)
        lg.append(_dec_logits(kp, qa, ea) * SCALE + taba_ref[p * PAGE:(p + 1) * PAGE, :])
        ksum.append(jnp.sum(kp, axis=0, keepdims=True))
    kmean = jnp.concatenate(
        [sum(ksum[n * pages_per_blk + i] for i in range(pages_per_blk)) * (1.0 / MOBA_BLOCK) for n in range(nblk)],
        axis=0)
    gate = _dec_logits(kmean, qa, ea, precision=HI)
    row = lax.broadcasted_iota(jnp.int32, gate.shape, 0)
    sel = _rank_lt(gate, row, 0, nblk, MOBA_TOPK)
    ka_new = _row8(z[:, A_K:A_K + wa])
    lg_self = (_dec_logits(ka_new, qa, ea) * SCALE)[0:1] + self_ref[0:1, :]
    lg = [jnp.where(sel[p // pages_per_blk:p // pages_per_blk + 1, :] > 0.5, x, NEG) for p, x in enumerate(lg)]
    m = lg_self
    for x in lg:
        m = jnp.maximum(m, jnp.max(x, axis=0, keepdims=True))
    e_self = jnp.exp(lg_self - m)
    den = e_self
    acc = [e * z[:, A_V:A_V + wa] for e in
           [jnp.dot(_row8(e_self).astype(MXU_DTYPE), et, preferred_element_type=F32)[0:1] for et in eta]]
    for p, x in enumerate(lg):
        e = jnp.exp(x - m)
        den = den + jnp.sum(e, axis=0, keepdims=True)
        pv = _dec_pv(e, _head_rows(ca[p], ga, ga, PAGE, rpp_a), eta)
        acc = [a + c for a, c in zip(acc, pv)]
    inv = 1.0 / den
    inv_r = [jnp.dot(_row8(inv), et.astype(F32), precision=HI, preferred_element_type=F32)[0:1] for et in eta]
    o_a = _heads_out([a * i for a, i in zip(acc, inv_r)], ga, ra)

    gb, rb = G_B, REP_B
    wb = gb * HEAD_DIM
    rpp_b = 4 * gb
    qb = _q_groups(z[:, B_Q:B_Q + H_B * HEAD_DIM], gb, rb)
    eb = [_head_lanes(gb, rb, r, False) for r in range(rb)]
    etb = [_head_lanes(gb, rb, r, True) for r in range(rb)]
    n_cmp = (past + 1 - CMP_LEN) // CMP_STRIDE + 1
    n_slc = -(-(past + 1) // SLC_BLOCK)

    def attend(parts, k_self, v_self, self_bias):
        lself = (_dec_logits(_row8(k_self), qb, eb) * SCALE)[0:1] + self_bias
        mm = lself
        for x, _ in parts:
            mm = jnp.maximum(mm, jnp.max(x, axis=0, keepdims=True))
        es = jnp.exp(lself - mm)
        dd = es
        ac = [jnp.dot(_row8(es).astype(MXU_DTYPE), et, preferred_element_type=F32)[0:1] * v_self for et in etb]
        for x, v in parts:
            ee = jnp.exp(x - mm)
            dd = dd + jnp.sum(ee, axis=0, keepdims=True)
            ac = [a + c for a, c in zip(ac, _dec_pv(ee, v, etb))]
        iv = 1.0 / dd
        iv_r = [jnp.dot(_row8(iv), et.astype(F32), precision=HI, preferred_element_type=F32)[0:1] for et in etb]
        return [a * i for a, i in zip(ac, iv_r)]

    for p in range(n_pages):
        for c in range(2 * gb):
            seq_sc[c, p * PAGE:(p + 1) * PAGE, :] = cb[p][pl.ds(c, PAGE, stride=rpp_b), :]
    kcmp = _compress_seq([seq_sc.at[g] for g in range(gb)], w1_ref, w2_ref, pe_ref, 0)
    vcmp = _compress_seq([seq_sc.at[gb + g] for g in range(gb)], w1_ref, w2_ref, pe_ref, 1)
    rowc = lax.broadcasted_iota(jnp.int32, (LANES, LANES), 0)
    okc = (rowc < n_cmp) & (rowc * CMP_STRIDE + CMP_LEN - 1 <= past)
    sc = jnp.where(okc, _dec_logits(kcmp, qb, eb) * SCALE, NEG)
    mc = jnp.max(sc, axis=0, keepdims=True)
    ec = jnp.where(okc, jnp.exp(sc - mc), 0.0)
    pc = ec / jnp.maximum(jnp.sum(ec, axis=0, keepdims=True), TINY)
    o_cmp = _dec_pv(pc, vcmp, etb)

    lane_i = lax.broadcasted_iota(jnp.int32, (LANES, LANES), 0)
    lane_o = lax.broadcasted_iota(jnp.int32, (LANES, LANES), 1)
    fold = ((lane_i < H_B) & ((lane_i & (gb - 1)) == lane_o)).astype(F32)
    unfold = ((lane_o < H_B) & ((lane_o & (gb - 1)) == lane_i)).astype(MXU_DTYPE)
    pg = jnp.dot(pc, fold, precision=HI, preferred_element_type=F32)
    imp = jnp.dot(_cover_matrix(n_cmp, n_slc, True), pg, precision=HI, preferred_element_type=F32)
    score, blk = _slc_scores(imp, jnp.full(imp.shape, past, jnp.int32), n_slc)
    selb = _rank_lt(score, blk, 0, n_slc, SLC_TOPK)
    selh = jnp.dot(selb.astype(MXU_DTYPE), unfold, preferred_element_type=F32).astype(MXU_DTYPE)
    blk_per_page = PAGE // SLC_BLOCK
    krow = lax.broadcasted_iota(jnp.int32, (PAGE, LANES), 0) >> _log2(SLC_BLOCK)
    kcol = lax.broadcasted_iota(jnp.int32, (PAGE, LANES), 1)
    parts = []
    for p in range(n_pages):
        expand = (kcol == krow + p * blk_per_page).astype(MXU_DTYPE)
        keep = jnp.dot(expand, selh, preferred_element_type=F32) > 0.5
        x = _dec_logits(_head_rows(cb[p], 2 * gb, gb, PAGE, rpp_b), qb, eb) * SCALE \
            + tabs_ref[p * PAGE:(p + 1) * PAGE, :]
        parts.append((jnp.where(keep, x, NEG), _head_rows(cb[p], 3 * gb, gb, PAGE, rpp_b)))
    o_sel = attend(parts, z[:, B_KS:B_KS + wb], z[:, B_VS:B_VS + wb], self_ref[1:2, :])

    nwin = win_ref.shape[0] // (2 * gb)
    xw = _dec_logits(_head_rows(win_ref, 0, gb, nwin, 2 * gb), qb, eb) * SCALE + tabw_ref[...]
    o_win = attend([(xw, _head_rows(win_ref, gb, gb, nwin, 2 * gb))],
                   z[:, B_KW:B_KW + wb], z[:, B_VW:B_VW + wb], self_ref[1:2, :])

    sg = jax.nn.sigmoid(z[:, B_GATE:B_GATE + LANES])
    lane1 = lax.broadcasted_iota(jnp.int32, (1, LANES), 1)
    o_b = []
    for g in range(gb):
        for r in range(rb):
            base = (g * rb + r) * 3
            hs = slice(g * HEAD_DIM, (g + 1) * HEAD_DIM)
            mix = jnp.zeros((1, HEAD_DIM), F32)
            for c, o in enumerate((o_cmp, o_sel, o_win)):
                gc = jnp.sum(jnp.where(lane1 == base + c, sg, 0.0), axis=1, keepdims=True)
                mix = mix + gc * o[r][:, hs]
            o_b.append(mix)
    o_ref[pl.ds(b, 1), :] = jnp.concatenate([o_a] + o_b, axis=1)


def _ab_sample(zs, cache_a, cache_b, win_buf, layer, page_table, pe_flat, w1, w2, tabs):
    db = zs.shape[0]
    n_pages = page_table.shape[1]
    n_even, n_pool = cache_a.shape[:2]
    rows_a = PAGE * 2 * H_KV_A
    rows_b = PAGE * 4 * G_B
    ca = cache_a.reshape(n_even, n_pool * rows_a, HEAD_DIM)
    cb = cache_b.reshape(n_even, n_pool * rows_b, HEAD_DIM)
    rows_w = win_buf.shape[2] * 2 * G_B
    win = win_buf.reshape(n_even, db, rows_w, HEAD_DIM)
    tab_a, tab_s, tab_w, tab_self = tabs
    width = (H_A + H_B) * HEAD_DIM

    def page_spec(rows, j):
        return pl.BlockSpec((None, rows, HEAD_DIM), lambda b, pt: (layer, pt[b, j], 0))

    def full(a):
        return pl.BlockSpec(a.shape, lambda b, pt: (0,) * a.ndim)

    grid_spec = pltpu.PrefetchScalarGridSpec(
        num_scalar_prefetch=1,
        grid=(db,),
        in_specs=([full(zs)]
                  + [page_spec(rows_a, j) for j in range(n_pages)]
                  + [page_spec(rows_b, j) for j in range(n_pages)]
                  + [pl.BlockSpec((None, None, rows_w, HEAD_DIM), lambda b, pt: (layer, b, 0, 0)),
                     full(pe_flat), full(w1), full(w2), full(tab_a), full(tab_s), full(tab_w), full(tab_self)]),
        out_specs=pl.BlockSpec((db, width), lambda b, pt: (0, 0)),
        scratch_shapes=[pltpu.VMEM((2 * G_B, n_pages * PAGE, HEAD_DIM), F32)],
    )
    return pl.pallas_call(
        functools.partial(_ab_sample_kernel, n_pages=n_pages),
        out_shape=jax.ShapeDtypeStruct((db, width), F32),
        grid_spec=grid_spec,
        compiler_params=_cparams(("arbitrary",)),
        name="ab_sample",
    )(page_table, zs, *([ca] * n_pages), *([cb] * n_pages), win, pe_flat, w1, w2, tab_a, tab_s, tab_w, tab_self)


def _sample_tables_ab(t5_bias, past, wbuf):
    t5a = _head_lane_table(t5_bias[:, :H_A], H_KV_A, REP_A)
    t5b = _head_lane_table(t5_bias[:, H_A:H_A + H_B], G_B, REP_B)
    tab_a = _row_table(t5a, past, past, 1)
    tab_s = _row_table(t5b, past, past, 1)
    tab_w = _row_table(t5b, wbuf, wbuf, 1)
    tab_self = jnp.pad(jnp.concatenate([t5a[0:1], t5b[0:1]], axis=0), ((0, SUBLANES - 2), (0, 0)))
    return tab_a, tab_s, tab_w, tab_self


def _dil_sample_kernel(z_ref, b1_ref, b2_ref, b3_ref, tab_ref, self_ref, o_ref):
    nh = H_C
    q = z_ref[0:nh, :]
    k_new = z_ref[nh:2 * nh, :]
    v_new = z_ref[2 * nh:3 * nh, :]
    ones = jnp.ones((HEAD_DIM, LANES), MXU_DTYPE)

    def lane_sum(x):
        return jnp.dot(x.astype(MXU_DTYPE), ones, preferred_element_type=F32)

    n_pat = len(C_PATTERNS)
    lself = lane_sum(k_new * q) * SCALE + self_ref[...]
    bufs = (b1_ref, b2_ref, b3_ref)
    lgs = []
    for i, buf in enumerate(bufs):
        prod = buf[:, 0:nh, :] * q[None]
        lg = lane_sum(prod.reshape(BAND * nh, HEAD_DIM)) * SCALE + tab_ref[i]
        lgs.append(lg.reshape(BAND, nh, LANES))
    m = lself
    for x in lgs:
        m = jnp.maximum(m, jnp.max(x, axis=0))
    es = jnp.exp(lself - m) * float(n_pat)
    den = es
    acc = es * v_new
    for x, buf in zip(lgs, bufs):
        ee = jnp.exp(x - m[None])
        den = den + jnp.sum(ee, axis=0)
        acc = acc + jnp.sum(ee * buf[:, nh:2 * nh, :], axis=0)
    o_ref[...] = acc / den


def _dil_sample(zs, buf_all, layer, tab, tab_self):
    db = zs.shape[0]
    n_odd, _, wc = buf_all.shape[:3]
    rpp = 2 * H_C
    z3 = zs.reshape(db, 3 * H_C, HEAD_DIM)
    specs, views = [], []
    for win, dil in C_PATTERNS:
        nrows = wc // dil
        views.append(buf_all.reshape(n_odd, db, nrows, dil, rpp, HEAD_DIM))
        specs.append(pl.BlockSpec((None, None, BAND, None, rpp, HEAD_DIM),
                                  lambda b, blk=nrows // BAND - 1: (layer, b, blk, 0, 0, 0)))
    out = pl.pallas_call(
        _dil_sample_kernel,
        out_shape=jax.ShapeDtypeStruct((db, H_C, HEAD_DIM), F32),
        grid=(db,),
        in_specs=[pl.BlockSpec((None, 3 * H_C, HEAD_DIM), lambda b: (b, 0, 0))] + specs
                 + [pl.BlockSpec(tab.shape, lambda b: (0, 0, 0)), pl.BlockSpec(tab_self.shape, lambda b: (0, 0))],
        out_specs=pl.BlockSpec((None, H_C, HEAD_DIM), lambda b: (b, 0, 0)),
        compiler_params=_cparams(("parallel",)),
        name="dilated_sample",
    )(z3, *views, tab, tab_self)
    return out.reshape(db, H_C * HEAD_DIM)


def _sample_tables_c(t5_bias):
    t5c = jnp.pad(t5_bias[:, :H_C], ((0, 0), (0, LANES - H_C)))
    shift = _log2(H_C)
    tabs = [_row_table(t5c, BAND * H_C, dil * BAND, dil, shift, True) for _, dil in C_PATTERNS]
    tab_self = jnp.broadcast_to(t5_bias[0, :H_C][:, None], (H_C, LANES))
    return jnp.stack(tabs), tab_self


def _pad_cols(w, to):
    return jnp.pad(w, ((0, 0), (0, to - w.shape[1])))


def kernel(x_prompt, x_sample, cache_a_kv, cache_b_kv, state_b_win, state_c_kv, page_table, t5_bias, norm_g,
           w_ffn_in, w_ffn_out, w_in_ab, w_out_ab, cmp_pe, w_cmp_1, w_cmp_2, w_in_c, w_out_c):
    b, s, d = x_prompt.shape
    db = x_sample.shape[0]
    depth = norm_g.shape[0]
    d_ff = w_ffn_out.shape[2]
    ffp = -(-d_ff // 512) * 512
    past = page_table.shape[1] * cache_a_kv.shape[2]
    assert x_sample.shape[1] == 1 and s % TQ == 0 and past % MOBA_BLOCK == 0
    assert state_c_kv.shape[2] == C_PATTERNS[-1][0] and state_b_win.shape[2] == WIN_B
    assert all(w // dl == BAND for w, dl in C_PATTERNS)

    xp = x_prompt.reshape(b * s, d)
    xs = x_sample.reshape(db, d)
    tm_ffn, tm_p, tm_s = 1024, 512, db

    def ffn_weights(layer, half):
        w_in = w_ffn_in[layer, half]
        w_in_p = jnp.concatenate([_pad_cols(w_in[:, :d_ff], ffp), _pad_cols(w_in[:, d_ff:], ffp)], axis=1)
        w_out_p = jnp.pad(w_ffn_out[layer, half], ((0, ffp - d_ff), (0, 0)))
        return w_in_p.astype(MXU_DTYPE), w_out_p.astype(MXU_DTYPE)

    outs = {}
    for layer in range(depth):
        i = layer // 2
        g = norm_g[layer]
        w_in_p, w_out_p = ffn_weights(layer, 0)
        xp = _ffn_half(xp, g[0], g[1], w_in_p, w_out_p, tm_ffn)
        xs = _ffn_half(xs, g[0], g[1], w_in_p, w_out_p, tm_s)
        if layer % 2 == 0:
            w_in = _pad_cols(w_in_ab[i], P_AB_PAD).astype(MXU_DTYPE)
            zp = _norm_matmul(xp, g[2], w_in, tm_p)
            zs = _norm_matmul(xs, g[2], w_in, tm_s)
            zp3 = zp.reshape(b, s, P_AB_PAD)
            near = [(0, NO_LIMIT, 1, 0), (TQ, NO_LIMIT, 1, 0), (0, NO_LIMIT, 1, 1)]
            tab_moba = _toeplitz_tables(t5_bias, 0, H_KV_A, REP_A, TQ, TQ, near)
            tab_sel = _toeplitz_tables(t5_bias, H_A, G_B, REP_B, TQ, TQ, near)
            tab_win = _toeplitz_tables(t5_bias, H_A, G_B, REP_B, TQ, TQ,
                                       [(d_ * TQ, WIN_B, 1, 0) for d_ in range(WIN_B // TQ + 1)])
            pe_flat = cmp_pe[i].reshape(2, 1, CMP_LEN * HEAD_DIM)
            w1 = w_cmp_1[i].astype(MXU_DTYPE)
            w2 = w_cmp_2[i].astype(MXU_DTYPE)
            o_a = _moba_prompt(zp3, tab_moba).reshape(b * s, H_A * HEAD_DIM)
            o_b = _nsa_prompt(zp3, pe_flat, w1, w2, tab_sel, tab_win).reshape(b * s, H_B * HEAD_DIM)
            w_out = w_out_ab[i].astype(MXU_DTYPE)
            xp = _out_proj(xp, o_a, o_b, 0, w_out, g[3], 256)
            o_s = _ab_sample(zs, cache_a_kv, cache_b_kv, state_b_win, i, page_table, pe_flat, w1, w2,
                             _sample_tables_ab(t5_bias, past, state_b_win.shape[2]))
            xs = _out_proj(xs, o_s, o_s, 1, w_out, g[3], db)
            outs.setdefault("a_p", []).append(zp3[:, :, A_K:B_Q].reshape(b, s, 2, H_KV_A, HEAD_DIM))
            outs.setdefault("a_s", []).append(zs[:, A_K:B_Q].reshape(db, 1, 2, H_KV_A, HEAD_DIM))
            outs.setdefault("b_p", []).append(zp3[:, :, B_KC:B_KW].reshape(b, s, 4, G_B, HEAD_DIM))
            outs.setdefault("b_s", []).append(zs[:, B_KC:B_KW].reshape(db, 1, 4, G_B, HEAD_DIM))
            nw = min(WIN_B, s)
            outs.setdefault("w_p", []).append(zp3[:, s - nw:, B_KW:B_GATE].reshape(b, nw, 2, G_B, HEAD_DIM))
            outs.setdefault("w_s", []).append(zs[:, B_KW:B_GATE].reshape(db, 1, 2, G_B, HEAD_DIM))
        else:
            w_in = w_in_c[i].astype(MXU_DTYPE)
            zp = _norm_matmul(xp, g[2], w_in, tm_p)
            zs = _norm_matmul(xs, g[2], w_in, tm_s)
            zp3 = zp.reshape(b, s, P_C)
            tab_dil = _toeplitz_tables(t5_bias, 0, H_C, 1, BAND, 2 * BAND,
                                       [(BAND, BAND, dl, 0) for _, dl in C_PATTERNS])
            parts = [_dil_prompt(zp3, tab_dil, pat, dl) for pat, (_, dl) in enumerate(C_PATTERNS)]
            o_c = _dil_combine(parts)
            w_out = w_out_c[i].astype(MXU_DTYPE)
            xp = _out_proj(xp, o_c, o_c, 1, w_out, g[3], 256)
            tab_c, self_c = _sample_tables_c(t5_bias)
            o_s = _dil_sample(zs, state_c_kv, i, tab_c, self_c)
            xs = _out_proj(xs, o_s, o_s, 1, w_out, g[3], db)
            hw = H_C * HEAD_DIM
            nc = min(C_PATTERNS[-1][0], s)
            outs.setdefault("c_p", []).append(zp3[:, s - nc:, hw:].reshape(b, nc, 2, H_C, HEAD_DIM))
            outs.setdefault("c_s", []).append(zs[:, hw:].reshape(db, 1, 2, H_C, HEAD_DIM))
        w_in_p, w_out_p = ffn_weights(layer, 1)
        xp = _ffn_half(xp, g[4], g[5], w_in_p, w_out_p, tm_ffn)
        xs = _ffn_half(xs, g[4], g[5], w_in_p, w_out_p, tm_s)
    return (xp.reshape(b, s, d), xs.reshape(db, 1, d)) + tuple(
        jnp.stack(outs[k]) for k in ("a_p", "a_s", "b_p", "b_s", "w_p", "w_s", "c_p", "c_s"))
```

```python
import functools
import math

import numpy as np
import jax
import jax.numpy as jnp
from jax import lax
from jax.experimental import pallas as pl
from jax.experimental.pallas import tpu as pltpu

F32 = jnp.float32
MXU_DTYPE = jnp.bfloat16
HI = lax.Precision.HIGHEST
NEG = -1e30
TINY = 1e-30
EPS = 1e-6
LANES = 128
SUBLANES = 8
VMEM_LIMIT = 56 * 1024 * 1024
HEAD_DIM = 128
SCALE = HEAD_DIM ** -0.5
N_BUCKETS = 32
T5_MAX_DIST = 128
H_A, H_KV_A, H_B, G_B, H_C = 8, 4, 8, 2, 16
REP_A = H_A // H_KV_A
REP_B = H_B // G_B
MOBA_BLOCK, MOBA_TOPK = 256, 3
CMP_LEN, CMP_STRIDE = 32, 16
SLC_BLOCK, SLC_TOPK = 64, 16
WIN_B = 512
C_PATTERNS = ((128, 1), (512, 4), (2048, 16))
PAGE = 128
TQ = 256
BAND = 128
NO_LIMIT = 1 << 30

A_Q, A_K, A_V, B_Q = 0, 1024, 1536, 2048
B_KC, B_VC, B_KS, B_VS, B_KW, B_VW, B_GATE = 3072, 3328, 3584, 3840, 4096, 4352, 4608
P_AB = 4632
P_AB_PAD = 5120
P_C = 6144

NT = (((1,), (1,)), ((), ()))


def _cparams(sem, vmem=VMEM_LIMIT):
    return pltpu.CompilerParams(dimension_semantics=sem, vmem_limit_bytes=vmem)


def _log2(n):
    assert n & (n - 1) == 0
    return n.bit_length() - 1


def _rms(x, g):
    return x * lax.rsqrt(jnp.mean(x * x, axis=-1, keepdims=True) + EPS) * g


def _ffn_kernel(x_ref, gpre_ref, gpost_ref, wg_ref, wu_ref, wo_ref, o_ref, h_sc):
    j = pl.program_id(1)

    @pl.when(j == 0)
    def _():
        h_sc[...] = _rms(x_ref[...], gpre_ref[...]).astype(h_sc.dtype)
        o_ref[...] = jnp.zeros_like(o_ref)

    h = h_sc[...]
    g = jnp.dot(h, wg_ref[...], preferred_element_type=F32)
    u = jnp.dot(h, wu_ref[...], preferred_element_type=F32)
    a = g * jax.nn.sigmoid(g) * u
    o_ref[...] += jnp.dot(a.astype(MXU_DTYPE), wo_ref[...], preferred_element_type=F32)

    @pl.when(j == pl.num_programs(1) - 1)
    def _():
        o_ref[...] = x_ref[...] + 0.5 * _rms(o_ref[...], gpost_ref[...])


def _ffn_half(x, g_pre, g_post, w_in_p, w_out_p, tm, tf=512):
    n, d = x.shape
    ffp = w_out_p.shape[0]
    nff = ffp // tf
    return pl.pallas_call(
        _ffn_kernel,
        out_shape=jax.ShapeDtypeStruct((n, d), F32),
        grid=(n // tm, nff),
        in_specs=[
            pl.BlockSpec((tm, d), lambda i, j: (i, 0)),
            pl.BlockSpec((1, d), lambda i, j: (0, 0)),
            pl.BlockSpec((1, d), lambda i, j: (0, 0)),
            pl.BlockSpec((d, tf), lambda i, j: (0, j)),
            pl.BlockSpec((d, tf), lambda i, j: (0, j + nff)),
            pl.BlockSpec((tf, d), lambda i, j: (j, 0)),
        ],
        out_specs=pl.BlockSpec((tm, d), lambda i, j: (i, 0)),
        scratch_shapes=[pltpu.VMEM((tm, d), MXU_DTYPE)],
        compiler_params=_cparams(("parallel", "arbitrary")),
        name="ffn_half",
    )(x, g_pre.reshape(1, d), g_post.reshape(1, d), w_in_p, w_in_p, w_out_p)


def _norm_mm_kernel(x_ref, g_ref, w_ref, *refs, starts):
    o_refs, h_sc = refs[:-1], refs[-1]
    j = pl.program_id(1)

    @pl.when(j == 0)
    def _():
        h_sc[...] = _rms(x_ref[...], g_ref[...]).astype(h_sc.dtype)

    y = jnp.dot(h_sc[...], w_ref[...], preferred_element_type=F32)
    for o_ref, lo, hi in zip(o_refs, starts[:-1], starts[1:]):
        @pl.when((j >= lo) & (j < hi))
        def _(o_ref=o_ref):
            o_ref[...] = y


def _norm_matmul(x, g, w, tm, splits, tn=1024):
    n, d = x.shape
    p = w.shape[1]
    starts = tuple(int(v) for v in np.cumsum((0,) + tuple(splits)))
    assert starts[-1] * tn == p

    def out_spec(lo, hi):
        return pl.BlockSpec((tm, tn), lambda i, j: (i, jnp.clip(j - lo, 0, hi - lo - 1)))

    return pl.pallas_call(
        functools.partial(_norm_mm_kernel, starts=starts),
        out_shape=tuple(jax.ShapeDtypeStruct((n, c * tn), F32) for c in splits),
        grid=(n // tm, p // tn),
        in_specs=[
            pl.BlockSpec((tm, d), lambda i, j: (i, 0)),
            pl.BlockSpec((1, d), lambda i, j: (0, 0)),
            pl.BlockSpec((d, tn), lambda i, j: (0, j)),
        ],
        out_specs=tuple(out_spec(lo, hi) for lo, hi in zip(starts[:-1], starts[1:])),
        scratch_shapes=[pltpu.VMEM((tm, d), MXU_DTYPE)],
        compiler_params=_cparams(("parallel", "arbitrary")),
        name="norm_matmul",
    )(x, g.reshape(1, d), w)


def _mm_norm_res_kernel(x_ref, o1_ref, o2_ref, w1_ref, w2_ref, g_ref, out_ref):
    y = jnp.dot(o1_ref[...].astype(MXU_DTYPE), w1_ref[...], preferred_element_type=F32)
    y = y + jnp.dot(o2_ref[...].astype(MXU_DTYPE), w2_ref[...], preferred_element_type=F32)
    out_ref[...] = x_ref[...] + _rms(y, g_ref[...])


def _out_proj(x, o1, o2, o2_col, w, g, tm):
    n, d = x.shape
    half = d // 2
    return pl.pallas_call(
        _mm_norm_res_kernel,
        out_shape=jax.ShapeDtypeStruct((n, d), F32),
        grid=(n // tm,),
        in_specs=[
            pl.BlockSpec((tm, d), lambda i: (i, 0)),
            pl.BlockSpec((tm, half), lambda i: (i, 0)),
            pl.BlockSpec((tm, half), lambda i: (i, o2_col)),
            pl.BlockSpec((half, d), lambda i: (0, 0)),
            pl.BlockSpec((half, d), lambda i: (1, 0)),
            pl.BlockSpec((1, d), lambda i: (0, 0)),
        ],
        out_specs=pl.BlockSpec((tm, d), lambda i: (i, 0)),
        compiler_params=_cparams(("parallel",)),
        name="out_proj",
    )(x, o1, o2, w, w, g.reshape(1, d))


def _bucket_vec(dist):
    exact = N_BUCKETS // 2
    d = jnp.maximum(dist, 1).astype(F32)
    far = exact + (jnp.log(d / exact) / math.log(T5_MAX_DIST / exact) * (N_BUCKETS - exact)).astype(jnp.int32)
    return jnp.where(dist < exact, dist, jnp.minimum(far, N_BUCKETS - 1))


def _toeplitz_kernel(par_ref, t5_ref, o_ref, *, rep, head0):
    kind = pl.program_id(1)
    h = head0 + pl.program_id(0) * rep + pl.program_id(2)
    off, max_back, scale, const = par_ref[kind, 0], par_ref[kind, 1], par_ref[kind, 2], par_ref[kind, 3]
    shape = o_ref.shape[2:]
    dist = off + lax.broadcasted_iota(jnp.int32, shape, 0) - lax.broadcasted_iota(jnp.int32, shape, 1)
    valid = ((dist >= 0) & (dist <= max_back)) | (const > 0)
    bkt = _bucket_vec(jnp.where(const > 0, T5_MAX_DIST, jnp.maximum(dist, 0) * scale))
    acc = jnp.zeros(shape, F32)
    for k in range(N_BUCKETS):
        acc = jnp.where(bkt == k, t5_ref[k, h], acc)
    o_ref[0, 0] = jnp.where(valid, acc, NEG)


def _toeplitz_tables(t5_bias, head0, n_groups, rep, rows, cols, kinds):
    par = jnp.asarray(np.asarray(kinds, np.int32))
    smem = pl.BlockSpec(memory_space=pltpu.SMEM)
    return pl.pallas_call(
        functools.partial(_toeplitz_kernel, rep=rep, head0=head0),
        out_shape=jax.ShapeDtypeStruct((n_groups, len(kinds), rep * rows, cols), F32),
        grid=(n_groups, len(kinds), rep),
        in_specs=[smem, smem],
        out_specs=pl.BlockSpec((1, 1, rows, cols), lambda g, k, r: (g, k, r, 0)),
        compiler_params=_cparams(("parallel", "parallel", "parallel")),
        name="bias_toeplitz",
    )(par, t5_bias)


def _row_table_kernel(t5p_ref, o_ref, *, base, step, shift, pick):
    shape = o_ref.shape
    row = lax.broadcasted_iota(jnp.int32, shape, 0)
    bkt = _bucket_vec(base - step * (row >> shift))
    acc = jnp.zeros(shape, F32)
    for k in range(N_BUCKETS):
        acc = jnp.where(bkt == k, t5p_ref[k:k + 1, :], acc)
    if pick:
        lane = lax.broadcasted_iota(jnp.int32, shape, 1)
        one = jnp.sum(jnp.where(lane == (row & ((1 << shift) - 1)), acc, 0.0), axis=1, keepdims=True)
        acc = jnp.broadcast_to(one, shape)
    o_ref[...] = acc


def _row_table(t5p, n, base, step, shift=0, pick=False):
    return pl.pallas_call(
        functools.partial(_row_table_kernel, base=base, step=step, shift=shift, pick=pick),
        out_shape=jax.ShapeDtypeStruct((n, LANES), F32),
        name="bias_rows",
    )(t5p)


def _head_lane_table(t5_heads, ng, rep):
    t = t5_heads.reshape(N_BUCKETS, ng, rep).transpose(0, 2, 1).reshape(N_BUCKETS, ng * rep)
    return jnp.pad(t, ((0, 0), (0, LANES - ng * rep)))


def _stack_heads(q, n):
    return jnp.concatenate([q[:, r * HEAD_DIM:(r + 1) * HEAD_DIM] for r in range(n)], axis=0)


def _unstack_heads(o, n):
    t = o.shape[0] // n
    return jnp.concatenate([o[r * t:(r + 1) * t] for r in range(n)], axis=1)


def _rank_lt(score, idx, axis, n_items, k):
    sel = jnp.zeros(score.shape, F32)
    for n in range(n_items):
        sn = score[:, n:n + 1] if axis == 1 else score[n:n + 1, :]
        beats = (score > sn) | ((score == sn) & (idx < n))
        rank = jnp.sum(beats.astype(F32), axis=axis, keepdims=True)
        sel = jnp.where((idx == n) & (rank < k), 1.0, sel)
    return sel


def _rows_to_cols(x_t):
    n, t = x_t.shape
    return jnp.concatenate([x_t, jnp.zeros((LANES - n, t), x_t.dtype)], axis=0).T


def _two_pass_attention(n_tiles, logits_fn, v_fn, s_sc, mx_sc, ls_sc, acc_sc):
    mx_sc[...] = jnp.full(mx_sc.shape, NEG, F32)

    def pass1(t, c):
        s = logits_fn(t)
        s_sc[t] = s
        mx_sc[...] = jnp.maximum(mx_sc[...], s)
        return c

    lax.fori_loop(0, n_tiles, pass1, 0)
    m = jnp.broadcast_to(jnp.max(mx_sc[...], axis=-1, keepdims=True), mx_sc.shape)
    ls_sc[...] = jnp.zeros_like(ls_sc)
    acc_sc[...] = jnp.zeros_like(acc_sc)

    def pass2(t, c):
        p = jnp.exp(s_sc[t] - m)
        ls_sc[...] += p
        acc_sc[...] += jnp.dot(p.astype(MXU_DTYPE), v_fn(t), preferred_element_type=F32)
        return c

    lax.fori_loop(0, n_tiles, pass2, 0)
    return acc_sc[...] / jnp.sum(ls_sc[...], axis=-1, keepdims=True)


def _moba_prompt_kernel(q_ref, k_ref, v_ref, tab_ref, o_ref, kmean_sc, s_sc, mx_sc, ls_sc, acc_sc, *, nblk):
    qi = pl.program_id(2)

    @pl.when(qi == 0)
    def _():
        kmean_sc[...] = jnp.zeros_like(kmean_sc)
        for n in range(nblk):
            blk = k_ref[0, n * MOBA_BLOCK:(n + 1) * MOBA_BLOCK, :]
            kmean_sc[n:n + 1, :] = jnp.sum(blk, axis=0, keepdims=True) * (1.0 / MOBA_BLOCK)

    qs = _stack_heads(q_ref[0], REP_A)
    nb8 = -(-nblk // SUBLANES) * SUBLANES
    gate_t = lax.dot_general(kmean_sc[...], qs, NT, precision=HI, preferred_element_type=F32)[0:nb8]
    blk_id = lax.broadcasted_iota(jnp.int32, gate_t.shape, 0)
    gate_t = jnp.where(blk_id < qi, gate_t, -jnp.inf)
    sel_t = _rank_lt(gate_t, blk_id, 0, nblk, MOBA_TOPK)
    sel_t = jnp.where(blk_id < qi, sel_t, 0.0)
    sel_t = jnp.where(blk_id == qi, 1.0, sel_t)
    sel = _rows_to_cols(sel_t).astype(MXU_DTYPE)

    qb = qs.astype(MXU_DTYPE)
    pick_row = lax.broadcasted_iota(jnp.int32, (LANES, MOBA_BLOCK), 0)

    def logits(j):
        start = pl.multiple_of(j * MOBA_BLOCK, MOBA_BLOCK)
        k = k_ref[0, pl.ds(start, MOBA_BLOCK), :].astype(MXU_DTYPE)
        s = lax.dot_general(qb, k, NT, preferred_element_type=F32) * SCALE + tab_ref[0, jnp.minimum(qi - j, 2)]
        keep = jnp.dot(sel, (pick_row == j).astype(MXU_DTYPE), preferred_element_type=F32)
        return jnp.where(keep > 0.5, s, NEG)

    def values(j):
        start = pl.multiple_of(j * MOBA_BLOCK, MOBA_BLOCK)
        return v_ref[0, pl.ds(start, MOBA_BLOCK), :].astype(MXU_DTYPE)

    out = _two_pass_attention(qi + 1, logits, values, s_sc, mx_sc, ls_sc, acc_sc)
    o_ref[0] = _unstack_heads(out, REP_A)


def _moba_prompt(q3, kv3, tab):
    b, s, _ = q3.shape
    nq = s // TQ
    rows = REP_A * TQ
    return pl.pallas_call(
        functools.partial(_moba_prompt_kernel, nblk=s // MOBA_BLOCK),
        out_shape=jax.ShapeDtypeStruct((b, s, H_A * HEAD_DIM), F32),
        grid=(b, H_KV_A, nq),
        in_specs=[
            pl.BlockSpec((1, TQ, REP_A * HEAD_DIM), lambda bi, h, qi: (bi, qi, h)),
            pl.BlockSpec((1, s, HEAD_DIM), lambda bi, h, qi: (bi, 0, h)),
            pl.BlockSpec((1, s, HEAD_DIM), lambda bi, h, qi: (bi, 0, H_KV_A + h)),
            pl.BlockSpec((1, 3, rows, TQ), lambda bi, h, qi: (h, 0, 0, 0)),
        ],
        out_specs=pl.BlockSpec((1, TQ, REP_A * HEAD_DIM), lambda bi, h, qi: (bi, qi, h)),
        scratch_shapes=[pltpu.VMEM((LANES, HEAD_DIM), F32), pltpu.VMEM((nq, rows, TQ), F32),
                        pltpu.VMEM((rows, TQ), F32), pltpu.VMEM((rows, TQ), F32),
                        pltpu.VMEM((rows, HEAD_DIM), F32)],
        compiler_params=_cparams(("parallel", "parallel", "arbitrary")),
        name="moba_prompt",
    )(q3, kv3, kv3, tab)


def _compress_seq(x_refs, w1_ref, w2_ref, pe_ref, slot):
    n_rows = 128
    ng = len(x_refs)
    acc_a = jnp.zeros((ng * n_rows, HEAD_DIM), F32)
    acc_b = jnp.zeros((ng * n_rows, HEAD_DIM), F32)
    for u in range(CMP_STRIDE):
        xs = jnp.concatenate([x[pl.ds(u, n_rows, stride=CMP_STRIDE), :] for x in x_refs], axis=0)
        xs = xs.astype(MXU_DTYPE)
        acc_a += jnp.dot(xs, w1_ref[slot, u * HEAD_DIM:(u + 1) * HEAD_DIM, :], preferred_element_type=F32)
        acc_b += jnp.dot(xs, w1_ref[slot, (CMP_STRIDE + u) * HEAD_DIM:(CMP_STRIDE + u + 1) * HEAD_DIM, :],
                         preferred_element_type=F32)
    pe8 = jnp.broadcast_to(pe_ref[slot], (SUBLANES, CMP_LEN * HEAD_DIM)).astype(MXU_DTYPE)
    c = jnp.dot(pe8, w1_ref[slot], preferred_element_type=F32)[0:1]
    outs = []
    for g in range(ng):
        a = acc_a[g * n_rows:(g + 1) * n_rows]
        bsh = pltpu.roll(acc_b[g * n_rows:(g + 1) * n_rows], n_rows - 1, 0)
        y = a + bsh + c
        hid = (y * jax.nn.sigmoid(y)).astype(MXU_DTYPE)
        outs.append(jnp.dot(hid, w2_ref[slot], preferred_element_type=F32))
    return jnp.concatenate(outs, axis=1) if ng > 1 else outs[0]


def _cover_matrix(n_cmp, n_slc, transposed):
    shape = (LANES, LANES)
    n = lax.broadcasted_iota(jnp.int32, shape, 1 if transposed else 0)
    s = lax.broadcasted_iota(jnp.int32, shape, 0 if transposed else 1)
    c = ((n * CMP_STRIDE < (s + 1) * SLC_BLOCK) & (n * CMP_STRIDE + CMP_LEN > s * SLC_BLOCK)
         & (n < n_cmp) & (s < n_slc))
    return c.astype(F32)


def _slc_scores(imp_t, pos, n_slc):
    blk = lax.broadcasted_iota(jnp.int32, imp_t.shape, 0)
    own = pos >> _log2(SLC_BLOCK)
    forced = (blk == 0) | (blk == own) | (blk == own - 1)
    score = jnp.where(forced, 1e30, jnp.where(blk > own, -1e30, imp_t))
    return jnp.where(blk < n_slc, score, -2e30), blk


def _nsa_prompt_kernel(q_ref, kc_ref, vc_ref, ks_ref, vs_ref, kw_ref, vw_ref, gate_ref, pe_ref, w1_ref,
                       w2_ref, tabs_ref, tabw_ref, o_ref, kcmp_sc, vcmp_sc, s_sc, mx_sc, ls_sc, acc_sc,
                       *, n_cmp, n_slc):
    g = pl.program_id(1)
    qi = pl.program_id(2)

    @pl.when(qi == 0)
    def _():
        kcmp_sc[...] = _compress_seq([kc_ref.at[0]], w1_ref, w2_ref, pe_ref, 0)
        vcmp_sc[...] = _compress_seq([vc_ref.at[0]], w1_ref, w2_ref, pe_ref, 1)

    rows = REP_B * TQ
    qs = _stack_heads(q_ref[0], REP_B)
    qb = qs.astype(MXU_DTYPE)

    s = lax.dot_general(qb, kcmp_sc[...].astype(MXU_DTYPE), NT, preferred_element_type=F32) * SCALE
    t_rows = qi * TQ + (lax.broadcasted_iota(jnp.int32, (rows, LANES), 0) & (TQ - 1))
    lane_rows = lax.broadcasted_iota(jnp.int32, (rows, LANES), 1)
    ok = (lane_rows * CMP_STRIDE + CMP_LEN - 1 <= t_rows) & (lane_rows < n_cmp)
    s = jnp.where(ok, s, NEG)
    m = jnp.max(s, axis=-1, keepdims=True)
    e = jnp.where(ok, jnp.exp(s - m), 0.0)
    p = e / jnp.maximum(jnp.sum(e, axis=-1, keepdims=True), TINY)
    o_cmp = jnp.dot(p.astype(MXU_DTYPE), vcmp_sc[...].astype(MXU_DTYPE), preferred_element_type=F32)

    psum = p[0:TQ]
    for r in range(1, REP_B):
        psum = psum + p[r * TQ:(r + 1) * TQ]
    imp_t = lax.dot_general(_cover_matrix(n_cmp, n_slc, True), psum, NT, precision=HI,
                            preferred_element_type=F32)
    ns8 = -(-n_slc // SUBLANES) * SUBLANES
    pos = qi * TQ + lax.broadcasted_iota(jnp.int32, (ns8, TQ), 1)
    score, blk = _slc_scores(imp_t[0:ns8], pos, n_slc)
    sel = _rows_to_cols(_rank_lt(score, blk, 0, n_slc, SLC_TOPK)).astype(MXU_DTYPE)

    blk_of_lane = lax.broadcasted_iota(jnp.int32, (LANES, TQ), 1) >> _log2(SLC_BLOCK)
    blk_row = lax.broadcasted_iota(jnp.int32, (LANES, TQ), 0)

    def sel_logits(j):
        start = pl.multiple_of(j * TQ, TQ)
        k = ks_ref[0, pl.ds(start, TQ), :].astype(MXU_DTYPE)
        expand = (blk_row == blk_of_lane + j * (TQ // SLC_BLOCK)).astype(MXU_DTYPE)
        keep1 = jnp.dot(sel, expand, preferred_element_type=F32)
        keep = jnp.concatenate([keep1] * REP_B, axis=0) > 0.5
        sc = lax.dot_general(qb, k, NT, preferred_element_type=F32) * SCALE + tabs_ref[0, jnp.minimum(qi - j, 2)]
        return jnp.where(keep, sc, NEG)

    def sel_values(j):
        return vs_ref[0, pl.ds(pl.multiple_of(j * TQ, TQ), TQ), :].astype(MXU_DTYPE)

    o_sel = _two_pass_attention(qi + 1, sel_logits, sel_values, s_sc, mx_sc, ls_sc, acc_sc)

    def win_logits(d):
        start = pl.multiple_of((qi - d) * TQ, TQ)
        k = kw_ref[0, pl.ds(start, TQ), :].astype(MXU_DTYPE)
        return lax.dot_general(qb, k, NT, preferred_element_type=F32) * SCALE + tabw_ref[0, d]

    def win_values(d):
        return vw_ref[0, pl.ds(pl.multiple_of((qi - d) * TQ, TQ), TQ), :].astype(MXU_DTYPE)

    o_win = _two_pass_attention(jnp.minimum(qi, WIN_B // TQ) + 1, win_logits, win_values,
                                s_sc, mx_sc, ls_sc, acc_sc)

    sg = jax.nn.sigmoid(gate_ref[0])
    lane = lax.broadcasted_iota(jnp.int32, (TQ, LANES), 1)
    outs = []
    for r in range(REP_B):
        base = (g * REP_B + r) * 3
        mix = jnp.zeros((TQ, HEAD_DIM), F32)
        for c, o in enumerate((o_cmp, o_sel, o_win)):
            gc = jnp.sum(jnp.where(lane == base + c, sg, 0.0), axis=1, keepdims=True)
            mix = mix + gc * o[r * TQ:(r + 1) * TQ]
        outs.append(mix)
    o_ref[0] = jnp.concatenate(outs, axis=1)


def _nsa_prompt(q3, kv3, wg3, pe_flat, w1, w2, tab_sel, tab_win):
    b, s, _ = q3.shape
    nq = s // TQ
    rows = REP_B * TQ
    n_cmp = (s - CMP_LEN) // CMP_STRIDE + 1
    n_slc = -(-s // SLC_BLOCK)

    def seq_spec(slot):
        return pl.BlockSpec((1, s, HEAD_DIM), lambda bi, g, qi: (bi, 0, slot * G_B + g))

    def full(a):
        return pl.BlockSpec(a.shape, lambda bi, g, qi: (0,) * a.ndim)

    qw = REP_B * HEAD_DIM
    return pl.pallas_call(
        functools.partial(_nsa_prompt_kernel, n_cmp=n_cmp, n_slc=n_slc),
        out_shape=jax.ShapeDtypeStruct((b, s, H_B * HEAD_DIM), F32),
        grid=(b, G_B, nq),
        in_specs=[
            pl.BlockSpec((1, TQ, qw), lambda bi, g, qi: (bi, qi, g)),
            seq_spec(0), seq_spec(1), seq_spec(2), seq_spec(3), seq_spec(0), seq_spec(1),
            pl.BlockSpec((1, TQ, LANES), lambda bi, g, qi: (bi, qi, 2 * G_B)),
            full(pe_flat), full(w1), full(w2),
            pl.BlockSpec((1, 3, rows, TQ), lambda bi, g, qi: (g, 0, 0, 0)),
            pl.BlockSpec((1, 3, rows, TQ), lambda bi, g, qi: (g, 0, 0, 0)),
        ],
        out_specs=pl.BlockSpec((1, TQ, qw), lambda bi, g, qi: (bi, qi, g)),
        scratch_shapes=[pltpu.VMEM((LANES, HEAD_DIM), F32), pltpu.VMEM((LANES, HEAD_DIM), F32),
                        pltpu.VMEM((nq, rows, TQ), F32), pltpu.VMEM((rows, TQ), F32),
                        pltpu.VMEM((rows, TQ), F32), pltpu.VMEM((rows, HEAD_DIM), F32)],
        compiler_params=_cparams(("parallel", "parallel", "arbitrary")),
        name="nsa_prompt",
    )(q3, kv3, kv3, kv3, kv3, wg3, wg3, wg3, pe_flat, w1, w2, tab_sel, tab_win)


def _rows(start, size, stride):
    return pl.ds(start, size) if stride == 1 else pl.ds(start, size, stride=stride)


def _dil_fused_kernel(q_ref, k_ref, v_ref, tab_ref, o_ref, m_sc, l_sc):
    s_len = q_ref.shape[1]
    for pat, (win, dil) in enumerate(C_PATTERNS):
        sf = s_len // dil
        for r in range(dil):
            for t in range(sf // BAND):
                rows = _rows(r + dil * BAND * t, BAND, dil)
                q = q_ref[0, rows, :].astype(MXU_DTYPE)
                if t == 0:
                    keys = rows
                    bias = tab_ref[0, pat, :, BAND:]
                else:
                    keys = _rows(r + dil * BAND * (t - 1), 2 * BAND, dil)
                    bias = tab_ref[0, pat]
                k = k_ref[0, keys, :].astype(MXU_DTYPE)
                v = v_ref[0, keys, :].astype(MXU_DTYPE)
                s = lax.dot_general(q, k, NT, preferred_element_type=F32) * SCALE + bias
                m_t = jnp.broadcast_to(jnp.max(s, axis=-1, keepdims=True), (BAND, LANES))
                if pat == 0:
                    m_new = m_t
                else:
                    m_old = m_sc[rows, :]
                    m_new = jnp.maximum(m_old, m_t)
                    alpha = jnp.exp(m_old - m_new)
                e = jnp.exp(s - jnp.concatenate([m_new] * (s.shape[1] // LANES), axis=1))
                l_t = jnp.broadcast_to(jnp.sum(e, axis=-1, keepdims=True), (BAND, LANES))
                pv = jnp.dot(e.astype(MXU_DTYPE), v, preferred_element_type=F32)
                if pat == 0:
                    l_sc[rows, :] = l_t
                    o_ref[0, rows, :] = pv
                else:
                    l_sc[rows, :] = alpha * l_sc[rows, :] + l_t
                    o_ref[0, rows, :] = alpha * o_ref[0, rows, :] + pv
                m_sc[rows, :] = m_new
    o_ref[0] = o_ref[0] / l_sc[...]


def _dil_fused(q3, kv3, tab):
    b, s, hw = q3.shape
    seq = pl.BlockSpec((1, s, HEAD_DIM), lambda bi, h: (bi, 0, h))
    return pl.pallas_call(
        _dil_fused_kernel,
        out_shape=jax.ShapeDtypeStruct((b, s, hw), F32),
        grid=(b, H_C),
        in_specs=[seq, seq, pl.BlockSpec((1, s, HEAD_DIM), lambda bi, h: (bi, 0, H_C + h)),
                  pl.BlockSpec((1,) + tab.shape[1:], lambda bi, h: (h, 0, 0, 0))],
        out_specs=seq,
        scratch_shapes=[pltpu.VMEM((s, LANES), F32), pltpu.VMEM((s, LANES), F32)],
        compiler_params=_cparams(("parallel", "parallel")),
        name="dilated_prompt",
    )(q3, kv3, kv3, tab)


def _head_lanes(ng, rep, r, transposed):
    shape = (LANES, ng * HEAD_DIM) if transposed else (ng * HEAD_DIM, LANES)
    c = lax.broadcasted_iota(jnp.int32, shape, 1 if transposed else 0)
    lane = lax.broadcasted_iota(jnp.int32, shape, 0 if transposed else 1)
    return (lane == r * ng + (c >> _log2(HEAD_DIM))).astype(MXU_DTYPE)


def _q_groups(q, ng, rep):
    return [jnp.concatenate([q[:, (g * rep + r) * HEAD_DIM:(g * rep + r + 1) * HEAD_DIM] for g in range(ng)],
                            axis=1) for r in range(rep)]


def _dec_logits(k, q_rs, e_rs, precision=None):
    out = None
    for q_r, e_r in zip(q_rs, e_rs):
        prod = k * q_r
        if precision is None:
            t = jnp.dot(prod.astype(MXU_DTYPE), e_r, preferred_element_type=F32)
        else:
            t = jnp.dot(prod, e_r.astype(F32), precision=precision, preferred_element_type=F32)
        out = t if out is None else out + t
    return out


def _dec_pv(p, v, et_rs):
    pb = p.astype(MXU_DTYPE)
    return [jnp.sum(jnp.dot(pb, et, preferred_element_type=F32) * v, axis=0, keepdims=True) for et in et_rs]


def _row8(x):
    return jnp.broadcast_to(x, (SUBLANES, x.shape[1]))


def _heads_out(o_rs, ng, rep):
    return jnp.concatenate([o_rs[r][:, g * HEAD_DIM:(g + 1) * HEAD_DIM] for g in range(ng) for r in range(rep)],
                           axis=1)


def _head_rows(ref, first, n_heads, n_rows, rows_per_pos):
    return jnp.concatenate([ref[pl.ds(first + h, n_rows, stride=rows_per_pos), :] for h in range(n_heads)], axis=1)


def _ab_sample_kernel(pt_ref, *refs, n_pages, n_z):
    z_refs, refs = refs[:n_z], refs[n_z:]
    ca = refs[:n_pages]
    cb = refs[n_pages:2 * n_pages]
    (win_ref, pe_ref, w1_ref, w2_ref, taba_ref, tabs_ref, tabw_ref, self_ref, o_ref, seq_sc) = refs[2 * n_pages:]
    del pt_ref
    b = pl.program_id(0)
    z = jnp.concatenate([r[pl.ds(b, 1), :] for r in z_refs], axis=1)
    past = n_pages * PAGE

    ga, ra = H_KV_A, REP_A
    wa = ga * HEAD_DIM
    rpp_a = 2 * ga
    qa = _q_groups(z[:, A_Q:A_Q + H_A * HEAD_DIM], ga, ra)
    ea = [_head_lanes(ga, ra, r, False) for r in range(ra)]
    eta = [_head_lanes(ga, ra, r, True) for r in range(ra)]
    pages_per_blk = MOBA_BLOCK // PAGE
    nblk = past // MOBA_BLOCK
    lg, ksum = [], []
    for p in range(n_pages):
        kp = _head_rows(ca[p], 0, ga, PAGE, rpp_a)
        lg.append(_dec_logits(kp, qa, ea) * SCALE + taba_ref[p * PAGE:(p + 1) * PAGE, :])
        ksum.append(jnp.sum(kp, axis=0, keepdims=True))
    kmean = jnp.concatenate(
        [sum(ksum[n * pages_per_blk + i] for i in range(pages_per_blk)) * (1.0 / MOBA_BLOCK) for n in range(nblk)],
        axis=0)
    gate = _dec_logits(kmean, qa, ea, precision=HI)
    row = lax.broadcasted_iota(jnp.int32, gate.shape, 0)
    sel = _rank_lt(gate, row, 0, nblk, MOBA_TOPK)
    ka_new = _row8(z[:, A_K:A_K + wa])
    lg_self = (_dec_logits(ka_new, qa, ea) * SCALE)[0:1] + self_ref[0:1, :]
    lg = [jnp.where(sel[p // pages_per_blk:p // pages_per_blk + 1, :] > 0.5, x, NEG) for p, x in enumerate(lg)]
    m = lg_self
    for x in lg:
        m = jnp.maximum(m, jnp.max(x, axis=0, keepdims=True))
    e_self = jnp.exp(lg_self - m)
    den = e_self
    acc = [e * z[:, A_V:A_V + wa] for e in
           [jnp.dot(_row8(e_self).astype(MXU_DTYPE), et, preferred_element_type=F32)[0:1] for et in eta]]
    for p, x in enumerate(lg):
        e = jnp.exp(x - m)
        den = den + jnp.sum(e, axis=0, keepdims=True)
        pv = _dec_pv(e, _head_rows(ca[p], ga, ga, PAGE, rpp_a), eta)
        acc = [a + c for a, c in zip(acc, pv)]
    inv = 1.0 / den
    inv_r = [jnp.dot(_row8(inv), et.astype(F32), precision=HI, preferred_element_type=F32)[0:1] for et in eta]
    o_a = _heads_out([a * i for a, i in zip(acc, inv_r)], ga, ra)

    gb, rb = G_B, REP_B
    wb = gb * HEAD_DIM
    rpp_b = 4 * gb
    qb = _q_groups(z[:, B_Q:B_Q + H_B * HEAD_DIM], gb, rb)
    eb = [_head_lanes(gb, rb, r, False) for r in range(rb)]
    etb = [_head_lanes(gb, rb, r, True) for r in range(rb)]
    n_cmp = (past + 1 - CMP_LEN) // CMP_STRIDE + 1
    n_slc = -(-(past + 1) // SLC_BLOCK)

    def attend(parts, k_self, v_self, self_bias):
        lself = (_dec_logits(_row8(k_self), qb, eb) * SCALE)[0:1] + self_bias
        mm = lself
        for x, _ in parts:
            mm = jnp.maximum(mm, jnp.max(x, axis=0, keepdims=True))
        es = jnp.exp(lself - mm)
        dd = es
        ac = [jnp.dot(_row8(es).astype(MXU_DTYPE), et, preferred_element_type=F32)[0:1] * v_self for et in etb]
        for x, v in parts:
            ee = jnp.exp(x - mm)
            dd = dd + jnp.sum(ee, axis=0, keepdims=True)
            ac = [a + c for a, c in zip(ac, _dec_pv(ee, v, etb))]
        iv = 1.0 / dd
        iv_r = [jnp.dot(_row8(iv), et.astype(F32), precision=HI, preferred_element_type=F32)[0:1] for et in etb]
        return [a * i for a, i in zip(ac, iv_r)]

    for p in range(n_pages):
        for c in range(2 * gb):
            seq_sc[c, p * PAGE:(p + 1) * PAGE, :] = cb[p][pl.ds(c, PAGE, stride=rpp_b), :]
    kcmp = _compress_seq([seq_sc.at[g] for g in range(gb)], w1_ref, w2_ref, pe_ref, 0)
    vcmp = _compress_seq([seq_sc.at[gb + g] for g in range(gb)], w1_ref, w2_ref, pe_ref, 1)
    rowc = lax.broadcasted_iota(jnp.int32, (LANES, LANES), 0)
    okc = (rowc < n_cmp) & (rowc * CMP_STRIDE + CMP_LEN - 1 <= past)
    sc = jnp.where(okc, _dec_logits(kcmp, qb, eb) * SCALE, NEG)
    mc = jnp.max(sc, axis=0, keepdims=True)
    ec = jnp.where(okc, jnp.exp(sc - mc), 0.0)
    pc = ec / jnp.maximum(jnp.sum(ec, axis=0, keepdims=True), TINY)
    o_cmp = _dec_pv(pc, vcmp, etb)

    lane_i = lax.broadcasted_iota(jnp.int32, (LANES, LANES), 0)
    lane_o = lax.broadcasted_iota(jnp.int32, (LANES, LANES), 1)
    fold = ((lane_i < H_B) & ((lane_i & (gb - 1)) == lane_o)).astype(F32)
    unfold = ((lane_o < H_B) & ((lane_o & (gb - 1)) == lane_i)).astype(MXU_DTYPE)
    pg = jnp.dot(pc, fold, precision=HI, preferred_element_type=F32)
    imp = jnp.dot(_cover_matrix(n_cmp, n_slc, True), pg, precision=HI, preferred_element_type=F32)
    score, blk = _slc_scores(imp, jnp.full(imp.shape, past, jnp.int32), n_slc)
    selb = _rank_lt(score, blk, 0, n_slc, SLC_TOPK)
    selh = jnp.dot(selb.astype(MXU_DTYPE), unfold, preferred_element_type=F32).astype(MXU_DTYPE)
    blk_per_page = PAGE // SLC_BLOCK
    krow = lax.broadcasted_iota(jnp.int32, (PAGE, LANES), 0) >> _log2(SLC_BLOCK)
    kcol = lax.broadcasted_iota(jnp.int32, (PAGE, LANES), 1)
    parts = []
    for p in range(n_pages):
        expand = (kcol == krow + p * blk_per_page).astype(MXU_DTYPE)
        keep = jnp.dot(expand, selh, preferred_element_type=F32) > 0.5
        x = _dec_logits(_head_rows(cb[p], 2 * gb, gb, PAGE, rpp_b), qb, eb) * SCALE \
            + tabs_ref[p * PAGE:(p + 1) * PAGE, :]
        parts.append((jnp.where(keep, x, NEG), _head_rows(cb[p], 3 * gb, gb, PAGE, rpp_b)))
    o_sel = attend(parts, z[:, B_KS:B_KS + wb], z[:, B_VS:B_VS + wb], self_ref[1:2, :])

    nwin = win_ref.shape[0] // (2 * gb)
    xw = _dec_logits(_head_rows(win_ref, 0, gb, nwin, 2 * gb), qb, eb) * SCALE + tabw_ref[...]
    o_win = attend([(xw, _head_rows(win_ref, gb, gb, nwin, 2 * gb))],
                   z[:, B_KW:B_KW + wb], z[:, B_VW:B_VW + wb], self_ref[1:2, :])

    sg = jax.nn.sigmoid(z[:, B_GATE:B_GATE + LANES])
    lane1 = lax.broadcasted_iota(jnp.int32, (1, LANES), 1)
    o_b = []
    for g in range(gb):
        for r in range(rb):
            base = (g * rb + r) * 3
            hs = slice(g * HEAD_DIM, (g + 1) * HEAD_DIM)
            mix = jnp.zeros((1, HEAD_DIM), F32)
            for c, o in enumerate((o_cmp, o_sel, o_win)):
                gc = jnp.sum(jnp.where(lane1 == base + c, sg, 0.0), axis=1, keepdims=True)
                mix = mix + gc * o[r][:, hs]
            o_b.append(mix)
    o_ref[pl.ds(b, 1), :] = jnp.concatenate([o_a] + o_b, axis=1)


def _ab_sample(zs, cache_a, cache_b, win_buf, layer, page_table, pe_flat, w1, w2, tabs):
    db = zs[0].shape[0]
    assert sum(z.shape[1] for z in zs) == P_AB_PAD
    n_pages = page_table.shape[1]
    n_even, n_pool = cache_a.shape[:2]
    rows_a = PAGE * 2 * H_KV_A
    rows_b = PAGE * 4 * G_B
    ca = cache_a.reshape(n_even, n_pool * rows_a, HEAD_DIM)
    cb = cache_b.reshape(n_even, n_pool * rows_b, HEAD_DIM)
    rows_w = win_buf.shape[2] * 2 * G_B
    win = win_buf.reshape(n_even, db, rows_w, HEAD_DIM)
    tab_a, tab_s, tab_w, tab_self = tabs
    width = (H_A + H_B) * HEAD_DIM

    def page_spec(rows, j):
        return pl.BlockSpec((None, rows, HEAD_DIM), lambda b, pt: (layer, pt[b, j], 0))

    def full(a):
        return pl.BlockSpec(a.shape, lambda b, pt: (0,) * a.ndim)

    grid_spec = pltpu.PrefetchScalarGridSpec(
        num_scalar_prefetch=1,
        grid=(db,),
        in_specs=([full(z) for z in zs]
                  + [page_spec(rows_a, j) for j in range(n_pages)]
                  + [page_spec(rows_b, j) for j in range(n_pages)]
                  + [pl.BlockSpec((None, None, rows_w, HEAD_DIM), lambda b, pt: (layer, b, 0, 0)),
                     full(pe_flat), full(w1), full(w2), full(tab_a), full(tab_s), full(tab_w), full(tab_self)]),
        out_specs=pl.BlockSpec((db, width), lambda b, pt: (0, 0)),
        scratch_shapes=[pltpu.VMEM((2 * G_B, n_pages * PAGE, HEAD_DIM), F32)],
    )
    return pl.pallas_call(
        functools.partial(_ab_sample_kernel, n_pages=n_pages, n_z=len(zs)),
        out_shape=jax.ShapeDtypeStruct((db, width), F32),
        grid_spec=grid_spec,
        compiler_params=_cparams(("arbitrary",)),
        name="ab_sample",
    )(page_table, *zs, *([ca] * n_pages), *([cb] * n_pages), win, pe_flat, w1, w2, tab_a, tab_s, tab_w, tab_self)


def _sample_tables_ab(t5_bias, past, wbuf):
    t5a = _head_lane_table(t5_bias[:, :H_A], H_KV_A, REP_A)
    t5b = _head_lane_table(t5_bias[:, H_A:H_A + H_B], G_B, REP_B)
    tab_a = _row_table(t5a, past, past, 1)
    tab_s = _row_table(t5b, past, past, 1)
    tab_w = _row_table(t5b, wbuf, wbuf, 1)
    tab_self = jnp.pad(jnp.concatenate([t5a[0:1], t5b[0:1]], axis=0), ((0, SUBLANES - 2), (0, 0)))
    return tab_a, tab_s, tab_w, tab_self


def _dil_sample_kernel(q_ref, kv_ref, b1_ref, b2_ref, b3_ref, tab_ref, self_ref, o_ref):
    nh = H_C
    q = q_ref[...]
    k_new = kv_ref[0:nh, :]
    v_new = kv_ref[nh:2 * nh, :]
    ones = jnp.ones((HEAD_DIM, LANES), MXU_DTYPE)

    def lane_sum(x):
        return jnp.dot(x.astype(MXU_DTYPE), ones, preferred_element_type=F32)

    n_pat = len(C_PATTERNS)
    lself = lane_sum(k_new * q) * SCALE + self_ref[...]
    bufs = (b1_ref, b2_ref, b3_ref)
    lgs = []
    for i, buf in enumerate(bufs):
        prod = buf[:, 0:nh, :] * q[None]
        lg = lane_sum(prod.reshape(BAND * nh, HEAD_DIM)) * SCALE + tab_ref[i]
        lgs.append(lg.reshape(BAND, nh, LANES))
    m = lself
    for x in lgs:
        m = jnp.maximum(m, jnp.max(x, axis=0))
    es = jnp.exp(lself - m) * float(n_pat)
    den = es
    acc = es * v_new
    for x, buf in zip(lgs, bufs):
        ee = jnp.exp(x - m[None])
        den = den + jnp.sum(ee, axis=0)
        acc = acc + jnp.sum(ee * buf[:, nh:2 * nh, :], axis=0)
    o_ref[...] = acc / den


def _dil_sample(qs, kvs, buf_all, layer, tab, tab_self):
    db = qs.shape[0]
    n_odd, _, wc = buf_all.shape[:3]
    rpp = 2 * H_C
    q3 = qs.reshape(db, H_C, HEAD_DIM)
    kv3 = kvs.reshape(db, rpp, HEAD_DIM)
    specs, views = [], []
    for win, dil in C_PATTERNS:
        nrows = wc // dil
        views.append(buf_all.reshape(n_odd, db, nrows, dil, rpp, HEAD_DIM))
        specs.append(pl.BlockSpec((None, None, BAND, None, rpp, HEAD_DIM),
                                  lambda b, blk=nrows // BAND - 1: (layer, b, blk, 0, 0, 0)))
    out = pl.pallas_call(
        _dil_sample_kernel,
        out_shape=jax.ShapeDtypeStruct((db, H_C, HEAD_DIM), F32),
        grid=(db,),
        in_specs=[pl.BlockSpec((None, H_C, HEAD_DIM), lambda b: (b, 0, 0)),
                  pl.BlockSpec((None, rpp, HEAD_DIM), lambda b: (b, 0, 0))] + specs
                 + [pl.BlockSpec(tab.shape, lambda b: (0, 0, 0)), pl.BlockSpec(tab_self.shape, lambda b: (0, 0))],
        out_specs=pl.BlockSpec((None, H_C, HEAD_DIM), lambda b: (b, 0, 0)),
        compiler_params=_cparams(("parallel",)),
        name="dilated_sample",
    )(q3, kv3, *views, tab, tab_self)
    return out.reshape(db, H_C * HEAD_DIM)


def _sample_tables_c(t5_bias):
    t5c = jnp.pad(t5_bias[:, :H_C], ((0, 0), (0, LANES - H_C)))
    shift = _log2(H_C)
    tabs = [_row_table(t5c, BAND * H_C, dil * BAND, dil, shift, True) for _, dil in C_PATTERNS]
    tab_self = jnp.broadcast_to(t5_bias[0, :H_C][:, None], (H_C, LANES))
    return jnp.stack(tabs), tab_self


def _pad_cols(w, to):
    return jnp.pad(w, ((0, 0), (0, to - w.shape[1])))


def kernel(x_prompt, x_sample, cache_a_kv, cache_b_kv, state_b_win, state_c_kv, page_table, t5_bias, norm_g,
           w_ffn_in, w_ffn_out, w_in_ab, w_out_ab, cmp_pe, w_cmp_1, w_cmp_2, w_in_c, w_out_c):
    b, s, d = x_prompt.shape
    db = x_sample.shape[0]
    depth = norm_g.shape[0]
    d_ff = w_ffn_out.shape[2]
    ffp = -(-d_ff // 512) * 512
    past = page_table.shape[1] * cache_a_kv.shape[2]
    assert x_sample.shape[1] == 1 and s % TQ == 0 and past % MOBA_BLOCK == 0
    assert state_c_kv.shape[2] == C_PATTERNS[-1][0] and state_b_win.shape[2] == WIN_B
    assert all(w // dl == BAND for w, dl in C_PATTERNS)

    xp = x_prompt.reshape(b * s, d)
    xs = x_sample.reshape(db, d)
    tm_ffn, tm_p, tm_s = 512, 512, db

    def ffn_weights(layer, half):
        w_in = w_ffn_in[layer, half]
        w_in_p = jnp.concatenate([_pad_cols(w_in[:, :d_ff], ffp), _pad_cols(w_in[:, d_ff:], ffp)], axis=1)
        w_out_p = jnp.pad(w_ffn_out[layer, half], ((0, ffp - d_ff), (0, 0)))
        return w_in_p.astype(MXU_DTYPE), w_out_p.astype(MXU_DTYPE)

    outs = {}
    for layer in range(depth):
        i = layer // 2
        g = norm_g[layer]
        w_in_p, w_out_p = ffn_weights(layer, 0)
        xp = _ffn_half(xp, g[0], g[1], w_in_p, w_out_p, tm_ffn)
        xs = _ffn_half(xs, g[0], g[1], w_in_p, w_out_p, tm_s)
        if layer % 2 == 0:
            w_in = _pad_cols(w_in_ab[i], P_AB_PAD).astype(MXU_DTYPE)
            ab_groups = (1,) * (P_AB_PAD // 1024)
            zp = [z.reshape(b, s, 1024) for z in _norm_matmul(xp, g[2], w_in, tm_p, ab_groups)]
            zs = _norm_matmul(xs, g[2], w_in, tm_s, ab_groups)
            near =[(0, NO_LIMIT, 1, 0), (TQ, NO_LIMIT, 1, 0), (0, NO_LIMIT, 1, 1)]
            tab_moba = _toeplitz_tables(t5_bias, 0, H_KV_A, REP_A, TQ, TQ, near)
            tab_sel = _toeplitz_tables(t5_bias, H_A, G_B, REP_B, TQ, TQ, near)
            tab_win = _toeplitz_tables(t5_bias, H_A, G_B, REP_B, TQ, TQ,
                                       [(d_ * TQ, WIN_B, 1, 0) for d_ in range(WIN_B // TQ + 1)])
            pe_flat = cmp_pe[i].reshape(2, 1, CMP_LEN * HEAD_DIM)
            w1 = w_cmp_1[i].astype(MXU_DTYPE)
            w2 = w_cmp_2[i].astype(MXU_DTYPE)
            o_a = _moba_prompt(zp[0], zp[1], tab_moba).reshape(b * s, H_A * HEAD_DIM)
            o_b = _nsa_prompt(zp[2], zp[3], zp[4], pe_flat, w1, w2, tab_sel, tab_win).reshape(b * s, H_B * HEAD_DIM)
            w_out = w_out_ab[i].astype(MXU_DTYPE)
            xp = _out_proj(xp, o_a, o_b, 0, w_out, g[3], 256)
            o_s = _ab_sample(zs, cache_a_kv, cache_b_kv, state_b_win, i, page_table, pe_flat, w1, w2,
                             _sample_tables_ab(t5_bias, past, state_b_win.shape[2]))
            xs = _out_proj(xs, o_s, o_s, 1, w_out, g[3], db)
            outs.setdefault("a_p", []).append(zp[1].reshape(b, s, 2, H_KV_A, HEAD_DIM))
            outs.setdefault("a_s", []).append(zs[1].reshape(db, 1, 2, H_KV_A, HEAD_DIM))
            outs.setdefault("b_p", []).append(zp[3].reshape(b, s, 4, G_B, HEAD_DIM))
            outs.setdefault("b_s", []).append(zs[3].reshape(db, 1, 4, G_B, HEAD_DIM))
            nw = min(WIN_B, s)
            ww = B_GATE - B_KW
            outs.setdefault("w_p", []).append(zp[4][:, s - nw:, :ww].reshape(b, nw, 2, G_B, HEAD_DIM))
            outs.setdefault("w_s", []).append(zs[4][:, :ww].reshape(db, 1, 2, G_B, HEAD_DIM))
        else:
            w_in = w_in_c[i].astype(MXU_DTYPE)
            hw = H_C * HEAD_DIM
            c_groups = (hw // 1024, 2 * hw // 1024)
            qp, kvp = _norm_matmul(xp, g[2], w_in, tm_p, c_groups)
            qs, kvs = _norm_matmul(xs, g[2], w_in, tm_s, c_groups)
            kvp3 = kvp.reshape(b, s, 2 * hw)
            tab_dil = _toeplitz_tables(t5_bias, 0, H_C, 1, BAND, 2 * BAND,
                                       [(BAND, BAND, dl, 0) for _, dl in C_PATTERNS])
            o_c = _dil_fused(qp.reshape(b, s, hw), kvp3, tab_dil).reshape(b * s, hw)
            w_out = w_out_c[i].astype(MXU_DTYPE)
            xp = _out_proj(xp, o_c, o_c, 1, w_out, g[3], 256)
            tab_c, self_c = _sample_tables_c(t5_bias)
            o_s = _dil_sample(qs, kvs, state_c_kv, i, tab_c, self_c)
            xs = _out_proj(xs, o_s, o_s, 1, w_out, g[3], db)
            nc = min(C_PATTERNS[-1][0], s)
            outs.setdefault("c_p", []).append(kvp3[:, s - nc:].reshape(b, nc, 2, H_C, HEAD_DIM))
            outs.setdefault("c_s", []).append(kvs.reshape(db, 1, 2, H_C, HEAD_DIM))
        w_in_p, w_out_p = ffn_weights(layer, 1)
        xp = _ffn_half(xp, g[4], g[5], w_in_p, w_out_p, tm_ffn)
        xs = _ffn_half(xs, g[4], g[5], w_in_p, w_out_p, tm_s)
    return (xp.reshape(b, s, d), xs.reshape(db, 1, d)) + tuple(
        jnp.stack(outs[k]) for k in ("a_p", "a_s", "b_p", "b_s", "w_p", "w_s", "c_p", "c_s"))
```

```python
import functools
import math

import numpy as np
import jax
import jax.numpy as jnp
from jax import lax
from jax.experimental import pallas as pl
from jax.experimental.pallas import tpu as pltpu

F32 = jnp.float32
MXU_DTYPE = jnp.bfloat16
HI = lax.Precision.HIGHEST
NEG = -1e30
TINY = 1e-30
EPS = 1e-6
LANES = 128
SUBLANES = 8
VMEM_LIMIT = 56 * 1024 * 1024
HEAD_DIM = 128
SCALE = HEAD_DIM ** -0.5
N_BUCKETS = 32
T5_MAX_DIST = 128
H_A, H_KV_A, H_B, G_B, H_C = 8, 4, 8, 2, 16
REP_A = H_A // H_KV_A
REP_B = H_B // G_B
MOBA_BLOCK, MOBA_TOPK = 256, 3
CMP_LEN, CMP_STRIDE = 32, 16
SLC_BLOCK, SLC_TOPK = 64, 16
WIN_B = 512
C_PATTERNS = ((128, 1), (512, 4), (2048, 16))
PAGE = 128
TQ = 256
BAND = 128
NO_LIMIT = 1 << 30

A_Q, A_K, A_V, B_Q = 0, 1024, 1536, 2048
B_KC, B_VC, B_KS, B_VS, B_KW, B_VW, B_GATE = 3072, 3328, 3584, 3840, 4096, 4352, 4608
P_AB = 4632
P_AB_PAD = 5120
P_C = 6144

NT = (((1,), (1,)), ((), ()))


def _cparams(sem, vmem=VMEM_LIMIT):
    return pltpu.CompilerParams(dimension_semantics=sem, vmem_limit_bytes=vmem)


def _log2(n):
    assert n & (n - 1) == 0
    return n.bit_length() - 1


def _rms(x, g):
    return x * lax.rsqrt(jnp.mean(x * x, axis=-1, keepdims=True) + EPS) * g


def _ffn_kernel(x_ref, gpre_ref, gpost_ref, wg_ref, wu_ref, wo_ref, o_ref, h_sc):
    j = pl.program_id(1)

    @pl.when(j == 0)
    def _():
        h_sc[...] = _rms(x_ref[...], gpre_ref[...]).astype(h_sc.dtype)
        o_ref[...] = jnp.zeros_like(o_ref)

    h = h_sc[...]
    g = jnp.dot(h, wg_ref[...], preferred_element_type=F32)
    u = jnp.dot(h, wu_ref[...], preferred_element_type=F32)
    a = g * jax.nn.sigmoid(g) * u
    o_ref[...] += jnp.dot(a.astype(MXU_DTYPE), wo_ref[...], preferred_element_type=F32)

    @pl.when(j == pl.num_programs(1) - 1)
    def _():
        o_ref[...] = x_ref[...] + 0.5 * _rms(o_ref[...], gpost_ref[...])


def _ffn_half(x, g_pre, g_post, weights, tm, tf=512):
    n, d = x.shape
    w_gate, w_up, w_out_p = weights
    ffp = w_out_p.shape[0]
    nff = ffp // tf
    return pl.pallas_call(
        _ffn_kernel,
        out_shape=jax.ShapeDtypeStruct((n, d), F32),
        grid=(n // tm, nff),
        in_specs=[
            pl.BlockSpec((tm, d), lambda i, j: (i, 0)),
            pl.BlockSpec((1, d), lambda i, j: (0, 0)),
            pl.BlockSpec((1, d), lambda i, j: (0, 0)),
            pl.BlockSpec((d, tf), lambda i, j: (0, j)),
            pl.BlockSpec((d, tf), lambda i, j: (0, j)),
            pl.BlockSpec((tf, d), lambda i, j: (j, 0)),
        ],
        out_specs=pl.BlockSpec((tm, d), lambda i, j: (i, 0)),
        scratch_shapes=[pltpu.VMEM((tm, d), MXU_DTYPE)],
        compiler_params=_cparams(("parallel", "arbitrary")),
        name="ffn_half",
    )(x, g_pre.reshape(1, d), g_post.reshape(1, d), w_gate, w_up, w_out_p)


def _norm_mm_kernel(x_ref, g_ref, w_ref, *refs, starts):
    o_refs, h_sc = refs[:-1], refs[-1]
    j = pl.program_id(1)

    @pl.when(j == 0)
    def _():
        h_sc[...] = _rms(x_ref[...], g_ref[...]).astype(h_sc.dtype)

    y = jnp.dot(h_sc[...], w_ref[...], preferred_element_type=F32)
    for o_ref, lo, hi in zip(o_refs, starts[:-1], starts[1:]):
        @pl.when((j >= lo) & (j < hi))
        def _(o_ref=o_ref):
            o_ref[...] = y


def _norm_matmul(x, g, w, tm, splits, tn=1024):
    n, d = x.shape
    p = w.shape[1]
    starts = tuple(int(v) for v in np.cumsum((0,) + tuple(splits)))
    assert starts[-1] * tn == p

    def out_spec(lo, hi):
        return pl.BlockSpec((tm, tn), lambda i, j: (i, jnp.clip(j - lo, 0, hi - lo - 1)))

    return pl.pallas_call(
        functools.partial(_norm_mm_kernel, starts=starts),
        out_shape=tuple(jax.ShapeDtypeStruct((n, c * tn), F32) for c in splits),
        grid=(n // tm, p // tn),
        in_specs=[
            pl.BlockSpec((tm, d), lambda i, j: (i, 0)),
            pl.BlockSpec((1, d), lambda i, j: (0, 0)),
            pl.BlockSpec((d, tn), lambda i, j: (0, j)),
        ],
        out_specs=tuple(out_spec(lo, hi) for lo, hi in zip(starts[:-1], starts[1:])),
        scratch_shapes=[pltpu.VMEM((tm, d), MXU_DTYPE)],
        compiler_params=_cparams(("parallel", "arbitrary")),
        name="norm_matmul",
    )(x, g.reshape(1, d), w)


def _mm_norm_res_kernel(x_ref, o1_ref, o2_ref, w1_ref, w2_ref, g_ref, out_ref):
    y = jnp.dot(o1_ref[...].astype(MXU_DTYPE), w1_ref[...], preferred_element_type=F32)
    y = y + jnp.dot(o2_ref[...].astype(MXU_DTYPE), w2_ref[...], preferred_element_type=F32)
    out_ref[...] = x_ref[...] + _rms(y, g_ref[...])


def _out_proj(x, o1, o2, o2_col, w, g, tm):
    n, d = x.shape
    half = d // 2
    return pl.pallas_call(
        _mm_norm_res_kernel,
        out_shape=jax.ShapeDtypeStruct((n, d), F32),
        grid=(n // tm,),
        in_specs=[
            pl.BlockSpec((tm, d), lambda i: (i, 0)),
            pl.BlockSpec((tm, half), lambda i: (i, 0)),
            pl.BlockSpec((tm, half), lambda i: (i, o2_col)),
            pl.BlockSpec((half, d), lambda i: (0, 0)),
            pl.BlockSpec((half, d), lambda i: (1, 0)),
            pl.BlockSpec((1, d), lambda i: (0, 0)),
        ],
        out_specs=pl.BlockSpec((tm, d), lambda i: (i, 0)),
        compiler_params=_cparams(("parallel",)),
        name="out_proj",
    )(x, o1, o2, w, w, g.reshape(1, d))


def _bucket_vec(dist):
    exact = N_BUCKETS // 2
    d = jnp.maximum(dist, 1).astype(F32)
    far = exact + (jnp.log(d / exact) / math.log(T5_MAX_DIST / exact) * (N_BUCKETS - exact)).astype(jnp.int32)
    return jnp.where(dist < exact, dist, jnp.minimum(far, N_BUCKETS - 1))


def _toeplitz_kernel(par_ref, t5_ref, o_ref, *, rep, head0):
    kind = pl.program_id(1)
    h = head0 + pl.program_id(0) * rep + pl.program_id(2)
    off, max_back, scale, const = par_ref[kind, 0], par_ref[kind, 1], par_ref[kind, 2], par_ref[kind, 3]
    shape = o_ref.shape[2:]
    dist = off + lax.broadcasted_iota(jnp.int32, shape, 0) - lax.broadcasted_iota(jnp.int32, shape, 1)
    valid = ((dist >= 0) & (dist <= max_back)) | (const > 0)
    bkt = _bucket_vec(jnp.where(const > 0, T5_MAX_DIST, jnp.maximum(dist, 0) * scale))
    acc = jnp.zeros(shape, F32)
    for k in range(N_BUCKETS):
        acc = jnp.where(bkt == k, t5_ref[k, h], acc)
    o_ref[0, 0] = jnp.where(valid, acc, NEG)


def _toeplitz_tables(t5_bias, head0, n_groups, rep, rows, cols, kinds):
    par = jnp.asarray(np.asarray(kinds, np.int32))
    smem = pl.BlockSpec(memory_space=pltpu.SMEM)
    return pl.pallas_call(
        functools.partial(_toeplitz_kernel, rep=rep, head0=head0),
        out_shape=jax.ShapeDtypeStruct((n_groups, len(kinds), rep * rows, cols), F32),
        grid=(n_groups, len(kinds), rep),
        in_specs=[smem, smem],
        out_specs=pl.BlockSpec((1, 1, rows, cols), lambda g, k, r: (g, k, r, 0)),
        compiler_params=_cparams(("parallel", "parallel", "parallel")),
        name="bias_toeplitz",
    )(par, t5_bias)


def _row_table_kernel(t5p_ref, o_ref, *, base, step, shift, pick):
    shape = o_ref.shape
    row = lax.broadcasted_iota(jnp.int32, shape, 0)
    bkt = _bucket_vec(base - step * (row >> shift))
    acc = jnp.zeros(shape, F32)
    for k in range(N_BUCKETS):
        acc = jnp.where(bkt == k, t5p_ref[k:k + 1, :], acc)
    if pick:
        lane = lax.broadcasted_iota(jnp.int32, shape, 1)
        one = jnp.sum(jnp.where(lane == (row & ((1 << shift) - 1)), acc, 0.0), axis=1, keepdims=True)
        acc = jnp.broadcast_to(one, shape)
    o_ref[...] = acc


def _row_table(t5p, n, base, step, shift=0, pick=False):
    return pl.pallas_call(
        functools.partial(_row_table_kernel, base=base, step=step, shift=shift, pick=pick),
        out_shape=jax.ShapeDtypeStruct((n, LANES), F32),
        name="bias_rows",
    )(t5p)


def _stack_heads(q, n):
    return jnp.concatenate([q[:, r * HEAD_DIM:(r + 1) * HEAD_DIM] for r in range(n)], axis=0)


def _unstack_heads(o, n):
    t = o.shape[0] // n
    return jnp.concatenate([o[r * t:(r + 1) * t] for r in range(n)], axis=1)


def _rank_lt(score, idx, axis, n_items, k):
    sel = jnp.zeros(score.shape, F32)
    for n in range(n_items):
        sn = score[:, n:n + 1] if axis == 1 else score[n:n + 1, :]
        beats = (score > sn) | ((score == sn) & (idx < n))
        rank = jnp.sum(beats.astype(F32), axis=axis, keepdims=True)
        sel = jnp.where((idx == n) & (rank < k), 1.0, sel)
    return sel


def _rows_to_cols(x_t):
    n, t = x_t.shape
    return jnp.concatenate([x_t, jnp.zeros((LANES - n, t), x_t.dtype)], axis=0).T


def _two_pass_attention(n_tiles, logits_fn, v_fn, s_sc, mx_sc, ls_sc, acc_sc):
    mx_sc[...] = jnp.full(mx_sc.shape, NEG, F32)

    def pass1(t, c):
        s = logits_fn(t)
        s_sc[t] = s
        mx_sc[...] = jnp.maximum(mx_sc[...], s)
        return c

    lax.fori_loop(0, n_tiles, pass1, 0)
    m = jnp.broadcast_to(jnp.max(mx_sc[...], axis=-1, keepdims=True), mx_sc.shape)
    ls_sc[...] = jnp.zeros_like(ls_sc)
    acc_sc[...] = jnp.zeros_like(acc_sc)

    def pass2(t, c):
        p = jnp.exp(s_sc[t] - m)
        ls_sc[...] += p
        acc_sc[...] += jnp.dot(p.astype(MXU_DTYPE), v_fn(t), preferred_element_type=F32)
        return c

    lax.fori_loop(0, n_tiles, pass2, 0)
    return acc_sc[...] / jnp.sum(ls_sc[...], axis=-1, keepdims=True)


def _moba_prompt_kernel(q_ref, k_ref, v_ref, tab_ref, o_ref, kmean_sc, s_sc, mx_sc, ls_sc, acc_sc, *, nblk):
    qi = pl.program_id(2)

    @pl.when(qi == 0)
    def _():
        kmean_sc[...] = jnp.zeros_like(kmean_sc)
        for n in range(nblk):
            blk = k_ref[0, n * MOBA_BLOCK:(n + 1) * MOBA_BLOCK, :]
            kmean_sc[n:n + 1, :] = jnp.sum(blk, axis=0, keepdims=True) * (1.0 / MOBA_BLOCK)

    qs = _stack_heads(q_ref[0], REP_A)
    nb8 = -(-nblk // SUBLANES) * SUBLANES
    gate_t = lax.dot_general(kmean_sc[...], qs, NT, precision=HI, preferred_element_type=F32)[0:nb8]
    blk_id = lax.broadcasted_iota(jnp.int32, gate_t.shape, 0)
    gate_t = jnp.where(blk_id < qi, gate_t, -jnp.inf)
    sel_t = _rank_lt(gate_t, blk_id, 0, nblk, MOBA_TOPK)
    sel_t = jnp.where(blk_id < qi, sel_t, 0.0)
    sel_t = jnp.where(blk_id == qi, 1.0, sel_t)
    sel = _rows_to_cols(sel_t).astype(MXU_DTYPE)

    qb = qs.astype(MXU_DTYPE)
    pick_row = lax.broadcasted_iota(jnp.int32, (LANES, MOBA_BLOCK), 0)

    def logits(j):
        start = pl.multiple_of(j * MOBA_BLOCK, MOBA_BLOCK)
        k = k_ref[0, pl.ds(start, MOBA_BLOCK), :].astype(MXU_DTYPE)
        s = lax.dot_general(qb, k, NT, preferred_element_type=F32) * SCALE + tab_ref[0, jnp.minimum(qi - j, 2)]
        keep = jnp.dot(sel, (pick_row == j).astype(MXU_DTYPE), preferred_element_type=F32)
        return jnp.where(keep > 0.5, s, NEG)

    def values(j):
        start = pl.multiple_of(j * MOBA_BLOCK, MOBA_BLOCK)
        return v_ref[0, pl.ds(start, MOBA_BLOCK), :].astype(MXU_DTYPE)

    out = _two_pass_attention(qi + 1, logits, values, s_sc, mx_sc, ls_sc, acc_sc)
    o_ref[0] = _unstack_heads(out, REP_A)


def _moba_prompt(q3, kv3, tab):
    b, s, _ = q3.shape
    nq = s // TQ
    rows = REP_A * TQ
    return pl.pallas_call(
        functools.partial(_moba_prompt_kernel, nblk=s // MOBA_BLOCK),
        out_shape=jax.ShapeDtypeStruct((b, s, H_A * HEAD_DIM), F32),
        grid=(b, H_KV_A, nq),
        in_specs=[
            pl.BlockSpec((1, TQ, REP_A * HEAD_DIM), lambda bi, h, qi: (bi, qi, h)),
            pl.BlockSpec((1, s, HEAD_DIM), lambda bi, h, qi: (bi, 0, h)),
            pl.BlockSpec((1, s, HEAD_DIM), lambda bi, h, qi: (bi, 0, H_KV_A + h)),
            pl.BlockSpec((1, 3, rows, TQ), lambda bi, h, qi: (h, 0, 0, 0)),
        ],
        out_specs=pl.BlockSpec((1, TQ, REP_A * HEAD_DIM), lambda bi, h, qi: (bi, qi, h)),
        scratch_shapes=[pltpu.VMEM((LANES, HEAD_DIM), F32), pltpu.VMEM((nq, rows, TQ), F32),
                        pltpu.VMEM((rows, TQ), F32), pltpu.VMEM((rows, TQ), F32),
                        pltpu.VMEM((rows, HEAD_DIM), F32)],
        compiler_params=_cparams(("parallel", "parallel", "arbitrary")),
        name="moba_prompt",
    )(q3, kv3, kv3, tab)


def _compress_seq(x_refs, w1_ref, w2_ref, pe_ref, slot):
    n_rows = 128
    ng = len(x_refs)
    acc_a = jnp.zeros((ng * n_rows, HEAD_DIM), F32)
    acc_b = jnp.zeros((ng * n_rows, HEAD_DIM), F32)
    for u in range(CMP_STRIDE):
        xs = jnp.concatenate([x[pl.ds(u, n_rows, stride=CMP_STRIDE), :] for x in x_refs], axis=0)
        xs = xs.astype(MXU_DTYPE)
        acc_a += jnp.dot(xs, w1_ref[slot, u * HEAD_DIM:(u + 1) * HEAD_DIM, :], preferred_element_type=F32)
        acc_b += jnp.dot(xs, w1_ref[slot, (CMP_STRIDE + u) * HEAD_DIM:(CMP_STRIDE + u + 1) * HEAD_DIM, :],
                         preferred_element_type=F32)
    pe8 = jnp.broadcast_to(pe_ref[slot], (SUBLANES, CMP_LEN * HEAD_DIM)).astype(MXU_DTYPE)
    c = jnp.dot(pe8, w1_ref[slot], preferred_element_type=F32)[0:1]
    outs = []
    for g in range(ng):
        a = acc_a[g * n_rows:(g + 1) * n_rows]
        bsh = pltpu.roll(acc_b[g * n_rows:(g + 1) * n_rows], n_rows - 1, 0)
        y = a + bsh + c
        hid = (y * jax.nn.sigmoid(y)).astype(MXU_DTYPE)
        outs.append(jnp.dot(hid, w2_ref[slot], preferred_element_type=F32))
    return jnp.concatenate(outs, axis=1) if ng > 1 else outs[0]


def _cover_matrix(n_cmp, n_slc, transposed):
    shape = (LANES, LANES)
    n = lax.broadcasted_iota(jnp.int32, shape, 1 if transposed else 0)
    s = lax.broadcasted_iota(jnp.int32, shape, 0 if transposed else 1)
    c = ((n * CMP_STRIDE < (s + 1) * SLC_BLOCK) & (n * CMP_STRIDE + CMP_LEN > s * SLC_BLOCK)
         & (n < n_cmp) & (s < n_slc))
    return c.astype(F32)


def _slc_scores(imp_t, pos, n_slc):
    blk = lax.broadcasted_iota(jnp.int32, imp_t.shape, 0)
    own = pos >> _log2(SLC_BLOCK)
    forced = (blk == 0) | (blk == own) | (blk == own - 1)
    score = jnp.where(forced, 1e30, jnp.where(blk > own, -1e30, imp_t))
    return jnp.where(blk < n_slc, score, -2e30), blk


def _nsa_prompt_kernel(q_ref, kc_ref, vc_ref, ks_ref, vs_ref, kw_ref, vw_ref, gate_ref, pe_ref, w1_ref,
                       w2_ref, tabs_ref, tabw_ref, o_ref, kcmp_sc, vcmp_sc, s_sc, mx_sc, ls_sc, acc_sc,
                       *, n_cmp, n_slc):
    g = pl.program_id(1)
    qi = pl.program_id(2)

    @pl.when(qi == 0)
    def _():
        kcmp_sc[...] = _compress_seq([kc_ref.at[0]], w1_ref, w2_ref, pe_ref, 0)
        vcmp_sc[...] = _compress_seq([vc_ref.at[0]], w1_ref, w2_ref, pe_ref, 1)

    rows = REP_B * TQ
    qs = _stack_heads(q_ref[0], REP_B)
    qb = qs.astype(MXU_DTYPE)

    s = lax.dot_general(qb, kcmp_sc[...].astype(MXU_DTYPE), NT, preferred_element_type=F32) * SCALE
    t_rows = qi * TQ + (lax.broadcasted_iota(jnp.int32, (rows, LANES), 0) & (TQ - 1))
    lane_rows = lax.broadcasted_iota(jnp.int32, (rows, LANES), 1)
    ok = (lane_rows * CMP_STRIDE + CMP_LEN - 1 <= t_rows) & (lane_rows < n_cmp)
    s = jnp.where(ok, s, NEG)
    m = jnp.max(s, axis=-1, keepdims=True)
    e = jnp.where(ok, jnp.exp(s - m), 0.0)
    p = e / jnp.maximum(jnp.sum(e, axis=-1, keepdims=True), TINY)
    o_cmp = jnp.dot(p.astype(MXU_DTYPE), vcmp_sc[...].astype(MXU_DTYPE), preferred_element_type=F32)

    psum = p[0:TQ]
    for r in range(1, REP_B):
        psum = psum + p[r * TQ:(r + 1) * TQ]
    imp_t = lax.dot_general(_cover_matrix(n_cmp, n_slc, True), psum, NT, precision=HI,
                            preferred_element_type=F32)
    ns8 = -(-n_slc // SUBLANES) * SUBLANES
    pos = qi * TQ + lax.broadcasted_iota(jnp.int32, (ns8, TQ), 1)
    score, blk = _slc_scores(imp_t[0:ns8], pos, n_slc)
    sel = _rows_to_cols(_rank_lt(score, blk, 0, n_slc, SLC_TOPK)).astype(MXU_DTYPE)

    blk_of_lane = lax.broadcasted_iota(jnp.int32, (LANES, TQ), 1) >> _log2(SLC_BLOCK)
    blk_row = lax.broadcasted_iota(jnp.int32, (LANES, TQ), 0)

    def sel_logits(j):
        start = pl.multiple_of(j * TQ, TQ)
        k = ks_ref[0, pl.ds(start, TQ), :].astype(MXU_DTYPE)
        expand = (blk_row == blk_of_lane + j * (TQ // SLC_BLOCK)).astype(MXU_DTYPE)
        keep1 = jnp.dot(sel, expand, preferred_element_type=F32)
        keep = jnp.concatenate([keep1] * REP_B, axis=0) > 0.5
        sc = lax.dot_general(qb, k, NT, preferred_element_type=F32) * SCALE + tabs_ref[0, jnp.minimum(qi - j, 2)]
        return jnp.where(keep, sc, NEG)

    def sel_values(j):
        return vs_ref[0, pl.ds(pl.multiple_of(j * TQ, TQ), TQ), :].astype(MXU_DTYPE)

    o_sel = _two_pass_attention(qi + 1, sel_logits, sel_values, s_sc, mx_sc, ls_sc, acc_sc)

    def win_logits(d):
        start = pl.multiple_of((qi - d) * TQ, TQ)
        k = kw_ref[0, pl.ds(start, TQ), :].astype(MXU_DTYPE)
        return lax.dot_general(qb, k, NT, preferred_element_type=F32) * SCALE + tabw_ref[0, d]

    def win_values(d):
        return vw_ref[0, pl.ds(pl.multiple_of((qi - d) * TQ, TQ), TQ), :].astype(MXU_DTYPE)

    o_win = _two_pass_attention(jnp.minimum(qi, WIN_B // TQ) + 1, win_logits, win_values,
                                s_sc, mx_sc, ls_sc, acc_sc)

    sg = jax.nn.sigmoid(gate_ref[0])
    lane = lax.broadcasted_iota(jnp.int32, (TQ, LANES), 1)
    outs = []
    for r in range(REP_B):
        base = (g * REP_B + r) * 3
        mix = jnp.zeros((TQ, HEAD_DIM), F32)
        for c, o in enumerate((o_cmp, o_sel, o_win)):
            gc = jnp.sum(jnp.where(lane == base + c, sg, 0.0), axis=1, keepdims=True)
            mix = mix + gc * o[r * TQ:(r + 1) * TQ]
        outs.append(mix)
    o_ref[0] = jnp.concatenate(outs, axis=1)


def _nsa_prompt(q3, kv3, wg3, pe_flat, w1, w2, tab_sel, tab_win):
    b, s, _ = q3.shape
    nq = s // TQ
    rows = REP_B * TQ
    n_cmp = (s - CMP_LEN) // CMP_STRIDE + 1
    n_slc = -(-s // SLC_BLOCK)

    def seq_spec(slot):
        return pl.BlockSpec((1, s, HEAD_DIM), lambda bi, g, qi: (bi, 0, slot * G_B + g))

    def full(a):
        return pl.BlockSpec(a.shape, lambda bi, g, qi: (0,) * a.ndim)

    qw = REP_B * HEAD_DIM
    return pl.pallas_call(
        functools.partial(_nsa_prompt_kernel, n_cmp=n_cmp, n_slc=n_slc),
        out_shape=jax.ShapeDtypeStruct((b, s, H_B * HEAD_DIM), F32),
        grid=(b, G_B, nq),
        in_specs=[
            pl.BlockSpec((1, TQ, qw), lambda bi, g, qi: (bi, qi, g)),
            seq_spec(0), seq_spec(1), seq_spec(2), seq_spec(3), seq_spec(0), seq_spec(1),
            pl.BlockSpec((1, TQ, LANES), lambda bi, g, qi: (bi, qi, 2 * G_B)),
            full(pe_flat), full(w1), full(w2),
            pl.BlockSpec((1, 3, rows, TQ), lambda bi, g, qi: (g, 0, 0, 0)),
            pl.BlockSpec((1, 3, rows, TQ), lambda bi, g, qi: (g, 0, 0, 0)),
        ],
        out_specs=pl.BlockSpec((1, TQ, qw), lambda bi, g, qi: (bi, qi, g)),
        scratch_shapes=[pltpu.VMEM((LANES, HEAD_DIM), F32), pltpu.VMEM((LANES, HEAD_DIM), F32),
                        pltpu.VMEM((nq, rows, TQ), F32), pltpu.VMEM((rows, TQ), F32),
                        pltpu.VMEM((rows, TQ), F32), pltpu.VMEM((rows, HEAD_DIM), F32)],
        compiler_params=_cparams(("parallel", "parallel", "arbitrary")),
        name="nsa_prompt",
    )(q3, kv3, kv3, kv3, kv3, wg3, wg3, wg3, pe_flat, w1, w2, tab_sel, tab_win)


def _rows(start, size, stride):
    return pl.ds(start, size) if stride == 1 else pl.ds(start, size, stride=stride)


def _dil_fused_kernel(q_ref, k_ref, v_ref, tab_ref, o_ref, m_sc, l_sc):
    s_len = q_ref.shape[1]
    for pat, (win, dil) in enumerate(C_PATTERNS):
        sf = s_len // dil
        for r in range(dil):
            for t in range(sf // BAND):
                rows = _rows(r + dil * BAND * t, BAND, dil)
                q = q_ref[0, rows, :].astype(MXU_DTYPE)
                if t == 0:
                    keys = rows
                    bias = tab_ref[0, pat, :, BAND:]
                else:
                    keys = _rows(r + dil * BAND * (t - 1), 2 * BAND, dil)
                    bias = tab_ref[0, pat]
                k = k_ref[0, keys, :].astype(MXU_DTYPE)
                v = v_ref[0, keys, :].astype(MXU_DTYPE)
                s = lax.dot_general(q, k, NT, preferred_element_type=F32) * SCALE + bias
                m_t = jnp.broadcast_to(jnp.max(s, axis=-1, keepdims=True), (BAND, LANES))
                if pat == 0:
                    m_new = m_t
                else:
                    m_old = m_sc[rows, :]
                    m_new = jnp.maximum(m_old, m_t)
                    alpha = jnp.exp(m_old - m_new)
                e = jnp.exp(s - jnp.concatenate([m_new] * (s.shape[1] // LANES), axis=1))
                l_t = jnp.broadcast_to(jnp.sum(e, axis=-1, keepdims=True), (BAND, LANES))
                pv = jnp.dot(e.astype(MXU_DTYPE), v, preferred_element_type=F32)
                if pat == 0:
                    l_sc[rows, :] = l_t
                    o_ref[0, rows, :] = pv
                else:
                    l_sc[rows, :] = alpha * l_sc[rows, :] + l_t
                    o_ref[0, rows, :] = alpha * o_ref[0, rows, :] + pv
                m_sc[rows, :] = m_new
    o_ref[0] = o_ref[0] / l_sc[...]


def _dil_fused(q3, kv3, tab):
    b, s, hw = q3.shape
    seq = pl.BlockSpec((1, s, HEAD_DIM), lambda bi, h: (bi, 0, h))
    return pl.pallas_call(
        _dil_fused_kernel,
        out_shape=jax.ShapeDtypeStruct((b, s, hw), F32),
        grid=(b, H_C),
        in_specs=[seq, seq, pl.BlockSpec((1, s, HEAD_DIM), lambda bi, h: (bi, 0, H_C + h)),
                  pl.BlockSpec((1,) + tab.shape[1:], lambda bi, h: (h, 0, 0, 0))],
        out_specs=seq,
        scratch_shapes=[pltpu.VMEM((s, LANES), F32), pltpu.VMEM((s, LANES), F32)],
        compiler_params=_cparams(("parallel", "parallel")),
        name="dilated_prompt",
    )(q3, kv3, kv3, tab)


QROWS = 16


def _ab_sample(zs, cache_a, cache_b, win_buf, layer, page_table, pe_flat, w1, w2, tabs):
    db = zs[0].shape[0]
    assert sum(z.shape[1] for z in zs) == P_AB_PAD
    n_pages = page_table.shape[1]
    n_even, n_pool = cache_a.shape[:2]
    rows_a = PAGE * 2 * H_KV_A
    rows_b = PAGE * 4 * G_B
    ca = cache_a.reshape(n_even, n_pool * rows_a, HEAD_DIM)
    cb = cache_b.reshape(n_even, n_pool * rows_b, HEAD_DIM)
    rows_w = win_buf.shape[2] * 2 * G_B
    win = win_buf.reshape(n_even, db, rows_w, HEAD_DIM)
    width = (H_A + H_B) * HEAD_DIM

    def page_spec(rows, j):
        return pl.BlockSpec((None, rows, HEAD_DIM), lambda b, pt: (layer, pt[b, j], 0))

    def full(a):
        return pl.BlockSpec(a.shape, lambda b, pt: (0,) * a.ndim)

    grid_spec = pltpu.PrefetchScalarGridSpec(
        num_scalar_prefetch=1,
        grid=(db,),
        in_specs=([full(z) for z in zs]
                  + [page_spec(rows_a, j) for j in range(n_pages)]
                  + [page_spec(rows_b, j) for j in range(n_pages)]
                  + [pl.BlockSpec((None, None, rows_w, HEAD_DIM), lambda b, pt: (layer, b, 0, 0)),
                     full(pe_flat), full(w1), full(w2)] + [full(t) for t in tabs]),
        out_specs=pl.BlockSpec((db, width), lambda b, pt: (0, 0)),
        scratch_shapes=[pltpu.VMEM((2 * G_B, n_pages * PAGE, HEAD_DIM), F32)],
    )
    return pl.pallas_call(
        functools.partial(_ab_sample_kernel, n_pages=n_pages, n_z=len(zs)),
        out_shape=jax.ShapeDtypeStruct((db, width), F32),
        grid_spec=grid_spec,
        compiler_params=_cparams(("arbitrary",)),
        name="ab_sample",
    )(page_table, *zs, *([ca] * n_pages), *([cb] * n_pages), win, pe_flat, w1, w2, *tabs)


def _rows_up(x, shift):
    x3 = x.reshape(x.shape[0] // SUBLANES, SUBLANES, x.shape[1])
    return pltpu.roll(x3, SUBLANES - shift, 1).reshape(x.shape)


def _stored_attention(q, blocks, k_to_v, tab, sel, blk_cols, lself, v_new, chunk=4):
    qb = q.astype(MXU_DTYPE)
    s = jnp.concatenate([lax.dot_general(qb, blk[...].astype(MXU_DTYPE), NT, preferred_element_type=F32)
                         for blk in blocks], axis=1) * SCALE + tab
    if sel is not None:
        n_cols = s.shape[1]
        s = jnp.concatenate([jnp.where(sel[:, n:n + 1] > 0.5, s[:, n * blk_cols:(n + 1) * blk_cols], NEG)
                             for n in range(n_cols // blk_cols)], axis=1)
    m = jnp.maximum(jnp.max(s, axis=1, keepdims=True), lself)
    e = jnp.exp(s - m).astype(MXU_DTYPE)
    es = jnp.exp(lself - m)
    den = jnp.sum(e.astype(F32), axis=1, keepdims=True) + es
    acc = es * v_new
    col = 0
    for i in range(0, len(blocks), chunk):
        vals = jnp.concatenate([_rows_up(blk[...], k_to_v).astype(MXU_DTYPE)
                                for blk in blocks[i:i + chunk]], axis=0)
        acc += jnp.dot(e[:, col:col + vals.shape[0]], vals, preferred_element_type=F32)
        col += vals.shape[0]
    return acc / den


def _rows_by(z, col0, idx_of_row, n):
    rows = [z[:, col0 + idx_of_row(r) * HEAD_DIM:col0 + (idx_of_row(r) + 1) * HEAD_DIM] for r in range(n)]
    return jnp.concatenate(rows + [jnp.zeros((QROWS - n, HEAD_DIM), F32)], axis=0)


def _ab_sample_kernel(pt_ref, *refs, n_pages, n_z):
    z_refs, refs = refs[:n_z], refs[n_z:]
    ca = refs[:n_pages]
    cb = refs[n_pages:2 * n_pages]
    (win_ref, pe_ref, w1_ref, w2_ref, taba_ref, tabs_ref, tabw_ref, selfa_ref, selfb_ref, o_ref, seq_sc) = \
        refs[2 * n_pages:]
    del pt_ref
    b = pl.program_id(0)
    z = jnp.concatenate([r[pl.ds(b, 1), :] for r in z_refs], axis=1)
    past = n_pages * PAGE
    row_id = lax.broadcasted_iota(jnp.int32, (QROWS, LANES), 0)
    lane_id = lax.broadcasted_iota(jnp.int32, (QROWS, LANES), 1)

    rpp_a = 2 * H_KV_A
    assert rpp_a == SUBLANES and 4 * G_B == SUBLANES
    pages_per_blk = MOBA_BLOCK // PAGE
    nblk = past // MOBA_BLOCK
    q = _rows_by(z, A_Q, lambda r: r, H_A)
    ksum = [jnp.sum(pg[...].reshape(PAGE, rpp_a, HEAD_DIM), axis=0) for pg in ca]
    kmean = [sum(ksum[n * pages_per_blk + i] for i in range(pages_per_blk)) * (1.0 / MOBA_BLOCK)
             for n in range(nblk)]
    kmean = jnp.concatenate(kmean + [jnp.zeros((LANES - nblk * rpp_a, HEAD_DIM), F32)], axis=0)
    g_all = lax.dot_general(q, kmean, NT, precision=HI, preferred_element_type=F32)
    gate = jnp.full((QROWS, LANES), -jnp.inf, F32)
    for n in range(nblk):
        pick = lane_id == n * rpp_a + (row_id >> _log2(REP_A))
        gate = jnp.where(lane_id == n, jnp.sum(jnp.where(pick, g_all, 0.0), axis=1, keepdims=True), gate)
    sel = _rank_lt(gate, lane_id, 1, nblk, MOBA_TOPK)
    lself = (jnp.sum(q * _rows_by(z, A_K, lambda r: r // REP_A, H_A), axis=1, keepdims=True) * SCALE
             + selfa_ref[:, 0:1])
    o_a = _stored_attention(q, ca, H_KV_A, taba_ref[...], sel, MOBA_BLOCK * rpp_a, lself,
                            _rows_by(z, A_V, lambda r: r // REP_A, H_A))

    rpp_b = 4 * G_B
    n_cmp = (past + 1 - CMP_LEN) // CMP_STRIDE + 1
    n_slc = -(-(past + 1) // SLC_BLOCK)
    own = past // SLC_BLOCK
    for p in range(n_pages):
        for c in range(2 * G_B):
            seq_sc[c, p * PAGE:(p + 1) * PAGE, :] = cb[p][pl.ds(c, PAGE, stride=rpp_b), :]
    kcmp = _compress_seq([seq_sc.at[g] for g in range(G_B)], w1_ref, w2_ref, pe_ref, 0)
    vcmp = _compress_seq([seq_sc.at[G_B + g] for g in range(G_B)], w1_ref, w2_ref, pe_ref, 1)
    q = _rows_by(z, B_Q, lambda r: r, H_B)
    qb = q.astype(MXU_DTYPE)
    grp = row_id >> _log2(REP_B)

    def by_group(fn):
        out = fn(0)
        for g in range(1, G_B):
            out = jnp.where(grp == g, fn(g), out)
        return out

    sc = by_group(lambda g: lax.dot_general(qb, kcmp[:, g * HEAD_DIM:(g + 1) * HEAD_DIM].astype(MXU_DTYPE), NT,
                                            preferred_element_type=F32)) * SCALE
    okc = (lane_id < n_cmp) & (lane_id * CMP_STRIDE + CMP_LEN - 1 <= past)
    sc = jnp.where(okc, sc, NEG)
    ec = jnp.where(okc, jnp.exp(sc - jnp.max(sc, axis=1, keepdims=True)), 0.0)
    pc = ec / jnp.maximum(jnp.sum(ec, axis=1, keepdims=True), TINY)
    pcb = pc.astype(MXU_DTYPE)
    o_cmp = by_group(lambda g: jnp.dot(pcb, vcmp[:, g * HEAD_DIM:(g + 1) * HEAD_DIM].astype(MXU_DTYPE),
                                       preferred_element_type=F32))
    psum = by_group(lambda g: jnp.sum(jnp.where((grp == g) & (row_id < H_B), pc, 0.0), axis=0, keepdims=True))
    imp = jnp.dot(psum, _cover_matrix(n_cmp, n_slc, False), precision=HI, preferred_element_type=F32)
    forced = (lane_id == 0) | (lane_id == own) | (lane_id == own - 1)
    score = jnp.where(forced, 1e30, jnp.where(lane_id > own, -1e30, imp))
    score = jnp.where(lane_id < n_slc, score, -2e30)
    sel = _rank_lt(score, lane_id, 1, n_slc, SLC_TOPK)
    lself = (jnp.sum(q * _rows_by(z, B_KS, lambda r: r // REP_B, H_B), axis=1, keepdims=True) * SCALE
             + selfb_ref[:, 0:1])
    o_sel = _stored_attention(q, cb, G_B, tabs_ref[...], sel, SLC_BLOCK * rpp_b, lself,
                              _rows_by(z, B_VS, lambda r: r // REP_B, H_B))
    lself = (jnp.sum(q * _rows_by(z, B_KW, lambda r: r // REP_B, H_B), axis=1, keepdims=True) * SCALE
             + selfb_ref[:, 0:1])
    o_win = _stored_attention(q, [win_ref], G_B, tabw_ref[...], None, 0, lself,
                              _rows_by(z, B_VW, lambda r: r // REP_B, H_B), chunk=1)
    sg = jax.nn.sigmoid(z[:, B_GATE:B_GATE + LANES])
    o_b = jnp.zeros((QROWS, HEAD_DIM), F32)
    for c, o in enumerate((o_cmp, o_sel, o_win)):
        o_b = o_b + jnp.sum(jnp.where(lane_id == row_id * 3 + c, sg, 0.0), axis=1, keepdims=True) * o
    o_ref[pl.ds(b, 1), :] = jnp.concatenate([o_a[r:r + 1] for r in range(H_A)] + [o_b[r:r + 1] for r in range(H_B)],
                                            axis=1)


def _sample_tables_ab(t5_bias, past, wbuf):
    def stored(t5_heads, n, rows_per_pos, key_row_of_head):
        nh = t5_heads.shape[1]
        base = _row_table(jnp.pad(t5_heads, ((0, 0), (0, LANES - nh))), n, n, 1)[:, :nh].T
        j = np.arange(rows_per_pos)[None, None, :]
        mine = j == np.asarray([key_row_of_head(h) for h in range(nh)])[:, None, None]
        tab = jnp.where(jnp.asarray(mine), base[:, :, None], NEG).reshape(nh, n * rows_per_pos)
        return jnp.pad(tab, ((0, QROWS - nh), (0, 0)))
    t5a, t5b = t5_bias[:, :H_A], t5_bias[:, H_A:H_A + H_B]
    tab_a = stored(t5a, past, 2 * H_KV_A, lambda h: h // REP_A)
    tab_s = stored(t5b, past, 4 * G_B, lambda h: 2 * G_B + h // REP_B)
    tab_w = stored(t5b, wbuf, 2 * G_B, lambda h: h // REP_B)
    self_a = jnp.broadcast_to(jnp.pad(t5a[0], (0, QROWS - H_A))[:, None], (QROWS, LANES))
    self_b = jnp.broadcast_to(jnp.pad(t5b[0], (0, QROWS - H_B))[:, None], (QROWS, LANES))
    return tab_a, tab_s, tab_w, self_a, self_b


def _dil_sample_kernel(q_ref, kv_ref, b1_ref, b2_ref, b3_ref, tab_ref, self_ref, o_ref):
    nh = H_C
    q = q_ref[...]
    k_new = kv_ref[0:nh, :]
    v_new = kv_ref[nh:2 * nh, :]
    ones = jnp.ones((HEAD_DIM, LANES), MXU_DTYPE)

    def lane_sum(x):
        return jnp.dot(x.astype(MXU_DTYPE), ones, preferred_element_type=F32)

    n_pat = len(C_PATTERNS)
    lself = lane_sum(k_new * q) * SCALE + self_ref[...]
    bufs = (b1_ref, b2_ref, b3_ref)
    lgs = []
    for i, buf in enumerate(bufs):
        prod = buf[:, 0:nh, :] * q[None]
        lg = lane_sum(prod.reshape(BAND * nh, HEAD_DIM)) * SCALE + tab_ref[i]
        lgs.append(lg.reshape(BAND, nh, LANES))
    m = lself
    for x in lgs:
        m = jnp.maximum(m, jnp.max(x, axis=0))
    es = jnp.exp(lself - m) * float(n_pat)
    den = es
    acc = es * v_new
    for x, buf in zip(lgs, bufs):
        ee = jnp.exp(x - m[None])
        den = den + jnp.sum(ee, axis=0)
        acc = acc + jnp.sum(ee * buf[:, nh:2 * nh, :], axis=0)
    o_ref[...] = acc / den


def _dil_sample(qs, kvs, buf_all, layer, tab, tab_self):
    db = qs.shape[0]
    n_odd, _, wc = buf_all.shape[:3]
    rpp = 2 * H_C
    q3 = qs.reshape(db, H_C, HEAD_DIM)
    kv3 = kvs.reshape(db, rpp, HEAD_DIM)
    specs, views = [], []
    for win, dil in C_PATTERNS:
        nrows = wc // dil
        views.append(buf_all.reshape(n_odd, db, nrows, dil, rpp, HEAD_DIM))
        specs.append(pl.BlockSpec((None, None, BAND, None, rpp, HEAD_DIM),
                                  lambda b, blk=nrows // BAND - 1: (layer, b, blk, 0, 0, 0)))
    out = pl.pallas_call(
        _dil_sample_kernel,
        out_shape=jax.ShapeDtypeStruct((db, H_C, HEAD_DIM), F32),
        grid=(db,),
        in_specs=[pl.BlockSpec((None, H_C, HEAD_DIM), lambda b: (b, 0, 0)),
                  pl.BlockSpec((None, rpp, HEAD_DIM), lambda b: (b, 0, 0))] + specs
                 + [pl.BlockSpec(tab.shape, lambda b: (0, 0, 0)), pl.BlockSpec(tab_self.shape, lambda b: (0, 0))],
        out_specs=pl.BlockSpec((None, H_C, HEAD_DIM), lambda b: (b, 0, 0)),
        compiler_params=_cparams(("parallel",)),
        name="dilated_sample",
    )(q3, kv3, *views, tab, tab_self)
    return out.reshape(db, H_C * HEAD_DIM)


def _sample_tables_c(t5_bias):
    t5c = jnp.pad(t5_bias[:, :H_C], ((0, 0), (0, LANES - H_C)))
    shift = _log2(H_C)
    tabs = [_row_table(t5c, BAND * H_C, dil * BAND, dil, shift, True) for _, dil in C_PATTERNS]
    tab_self = jnp.broadcast_to(t5_bias[0, :H_C][:, None], (H_C, LANES))
    return jnp.stack(tabs), tab_self


def _pad_cols(w, to):
    return jnp.pad(w, ((0, 0), (0, to - w.shape[1])))


def kernel(x_prompt, x_sample, cache_a_kv, cache_b_kv, state_b_win, state_c_kv, page_table, t5_bias, norm_g,
           w_ffn_in, w_ffn_out, w_in_ab, w_out_ab, cmp_pe, w_cmp_1, w_cmp_2, w_in_c, w_out_c):
    b, s, d = x_prompt.shape
    db = x_sample.shape[0]
    depth = norm_g.shape[0]
    d_ff = w_ffn_out.shape[2]
    ffp = -(-d_ff // 512) * 512
    past = page_table.shape[1] * cache_a_kv.shape[2]
    assert x_sample.shape[1] == 1 and s % TQ == 0 and past % MOBA_BLOCK == 0
    assert state_c_kv.shape[2] == C_PATTERNS[-1][0] and state_b_win.shape[2] == WIN_B
    assert all(w // dl == BAND for w, dl in C_PATTERNS)

    xp = x_prompt.reshape(b * s, d)
    xs = x_sample.reshape(db, d)
    tm_ffn, tm_p, tm_s = 512, 512, db

    def ffn_weights(layer, half):
        w_in = w_ffn_in[layer, half]
        w_gate = _pad_cols(w_in[:, :d_ff].astype(MXU_DTYPE), ffp)
        w_up = _pad_cols(w_in[:, d_ff:].astype(MXU_DTYPE), ffp)
        w_out_p = jnp.pad(w_ffn_out[layer, half].astype(MXU_DTYPE), ((0, ffp - d_ff), (0, 0)))
        return w_gate, w_up, w_out_p

    outs = {}
    for layer in range(depth):
        i = layer // 2
        g = norm_g[layer]
        w_ffn = ffn_weights(layer, 0)
        xp = _ffn_half(xp, g[0], g[1], w_ffn, tm_ffn)
        xs = _ffn_half(xs, g[0], g[1], w_ffn, tm_s)
        if layer % 2 == 0:
            w_in = _pad_cols(w_in_ab[i], P_AB_PAD).astype(MXU_DTYPE)
            ab_groups = (1,) * (P_AB_PAD // 1024)
            zp = [z.reshape(b, s, 1024) for z in _norm_matmul(xp, g[2], w_in, tm_p, ab_groups)]
            zs = _norm_matmul(xs, g[2], w_in, tm_s, ab_groups)
            near =[(0, NO_LIMIT, 1, 0), (TQ, NO_LIMIT, 1, 0), (0, NO_LIMIT, 1, 1)]
            tab_moba = _toeplitz_tables(t5_bias, 0, H_KV_A, REP_A, TQ, TQ, near)
            tab_sel = _toeplitz_tables(t5_bias, H_A, G_B, REP_B, TQ, TQ, near)
            tab_win = _toeplitz_tables(t5_bias, H_A, G_B, REP_B, TQ, TQ,
                                       [(d_ * TQ, WIN_B, 1, 0) for d_ in range(WIN_B // TQ + 1)])
            pe_flat = cmp_pe[i].reshape(2, 1, CMP_LEN * HEAD_DIM)
            w1 = w_cmp_1[i].astype(MXU_DTYPE)
            w2 = w_cmp_2[i].astype(MXU_DTYPE)
            o_a = _moba_prompt(zp[0], zp[1], tab_moba).reshape(b * s, H_A * HEAD_DIM)
            o_b = _nsa_prompt(zp[2], zp[3], zp[4], pe_flat, w1, w2, tab_sel, tab_win).reshape(b * s, H_B * HEAD_DIM)
            w_out = w_out_ab[i].astype(MXU_DTYPE)
            xp = _out_proj(xp, o_a, o_b, 0, w_out, g[3], tm_p)
            o_s = _ab_sample(zs, cache_a_kv, cache_b_kv, state_b_win, i, page_table, pe_flat, w1, w2,
                             _sample_tables_ab(t5_bias, past, state_b_win.shape[2]))
            xs = _out_proj(xs, o_s, o_s, 1, w_out, g[3], db)
            outs.setdefault("a_p", []).append(zp[1].reshape(b, s, 2, H_KV_A, HEAD_DIM))
            outs.setdefault("a_s", []).append(zs[1].reshape(db, 1, 2, H_KV_A, HEAD_DIM))
            outs.setdefault("b_p", []).append(zp[3].reshape(b, s, 4, G_B, HEAD_DIM))
            outs.setdefault("b_s", []).append(zs[3].reshape(db, 1, 4, G_B, HEAD_DIM))
            nw = min(WIN_B, s)
            ww = B_GATE - B_KW
            outs.setdefault("w_p", []).append(zp[4][:, s - nw:, :ww].reshape(b, nw, 2, G_B, HEAD_DIM))
            outs.setdefault("w_s", []).append(zs[4][:, :ww].reshape(db, 1, 2, G_B, HEAD_DIM))
        else:
            w_in = w_in_c[i].astype(MXU_DTYPE)
            hw = H_C * HEAD_DIM
            c_groups = (hw // 1024, 2 * hw // 1024)
            qp, kvp = _norm_matmul(xp, g[2], w_in, tm_p, c_groups)
            qs, kvs = _norm_matmul(xs, g[2], w_in, tm_s, c_groups)
            kvp3 = kvp.reshape(b, s, 2 * hw)
            tab_dil = _toeplitz_tables(t5_bias, 0, H_C, 1, BAND, 2 * BAND,
                                       [(BAND, BAND, dl, 0) for _, dl in C_PATTERNS])
            o_c = _dil_fused(qp.reshape(b, s, hw), kvp3, tab_dil).reshape(b * s, hw)
            w_out = w_out_c[i].astype(MXU_DTYPE)
            xp = _out_proj(xp, o_c, o_c, 1, w_out, g[3], tm_p)
            tab_c, self_c = _sample_tables_c(t5_bias)
            o_s = _dil_sample(qs, kvs, state_c_kv, i, tab_c, self_c)
            xs = _out_proj(xs, o_s, o_s, 1, w_out, g[3], db)
            nc = min(C_PATTERNS[-1][0], s)
            outs.setdefault("c_p", []).append(kvp3[:, s - nc:].reshape(b, nc, 2, H_C, HEAD_DIM))
            outs.setdefault("c_s", []).append(kvs.reshape(db, 1, 2, H_C, HEAD_DIM))
        w_ffn = ffn_weights(layer, 1)
        xp = _ffn_half(xp, g[4], g[5], w_ffn, tm_ffn)
        xs = _ffn_half(xs, g[4], g[5], w_ffn, tm_s)
    return (xp.reshape(b, s, d), xs.reshape(db, 1, d)) + tuple(
        jnp.stack(outs[k]) for k in ("a_p", "a_s", "b_p", "b_s", "w_p", "w_s", "c_p", "c_s"))
```

```python
import functools
import math

import numpy as np
import jax
import jax.numpy as jnp
from jax import lax
from jax.experimental import pallas as pl
from jax.experimental.pallas import tpu as pltpu

F32 = jnp.float32
MXU_DTYPE = jnp.bfloat16
HI = lax.Precision.HIGHEST
NEG = -1e30
TINY = 1e-30
EPS = 1e-6
LANES = 128
SUBLANES = 8
VMEM_LIMIT = 56 * 1024 * 1024
HEAD_DIM = 128
SCALE = HEAD_DIM ** -0.5
N_BUCKETS = 32
T5_MAX_DIST = 128
H_A, H_KV_A, H_B, G_B, H_C = 8, 4, 8, 2, 16
REP_A = H_A // H_KV_A
REP_B = H_B // G_B
MOBA_BLOCK, MOBA_TOPK = 256, 3
CMP_LEN, CMP_STRIDE = 32, 16
SLC_BLOCK, SLC_TOPK = 64, 16
WIN_B = 512
C_PATTERNS = ((128, 1), (512, 4), (2048, 16))
PAGE = 128
TQ = 256
BAND = 128
NO_LIMIT = 1 << 30

A_Q, A_K, A_V, B_Q = 0, 1024, 1536, 2048
B_KC, B_VC, B_KS, B_VS, B_KW, B_VW, B_GATE = 3072, 3328, 3584, 3840, 4096, 4352, 4608
P_AB = 4632
P_AB_PAD = 5120
P_C = 6144

NT = (((1,), (1,)), ((), ()))


def _cparams(sem, vmem=VMEM_LIMIT):
    return pltpu.CompilerParams(dimension_semantics=sem, vmem_limit_bytes=vmem)


def _log2(n):
    assert n & (n - 1) == 0
    return n.bit_length() - 1


def _rms(x, g):
    return x * lax.rsqrt(jnp.mean(x * x, axis=-1, keepdims=True) + EPS) * g


def _ffn_kernel(x_ref, gpre_ref, gpost_ref, wg_ref, wu_ref, wo_ref, o_ref, h_sc):
    j = pl.program_id(1)

    @pl.when(j == 0)
    def _():
        h_sc[...] = _rms(x_ref[...], gpre_ref[...]).astype(h_sc.dtype)
        o_ref[...] = jnp.zeros_like(o_ref)

    h = h_sc[...]
    g = jnp.dot(h, wg_ref[...], preferred_element_type=F32)
    u = jnp.dot(h, wu_ref[...], preferred_element_type=F32)
    a = g * jax.nn.sigmoid(g) * u
    o_ref[...] += jnp.dot(a.astype(MXU_DTYPE), wo_ref[...], preferred_element_type=F32)

    @pl.when(j == pl.num_programs(1) - 1)
    def _():
        o_ref[...] = x_ref[...] + 0.5 * _rms(o_ref[...], gpost_ref[...])


def _ffn_half(x, g_pre, g_post, weights, tm, tf=512):
    n, d = x.shape
    w_gate, w_up, w_out_p = weights
    ffp = w_out_p.shape[0]
    nff = ffp // tf
    return pl.pallas_call(
        _ffn_kernel,
        out_shape=jax.ShapeDtypeStruct((n, d), F32),
        grid=(n // tm, nff),
        in_specs=[
            pl.BlockSpec((tm, d), lambda i, j: (i, 0)),
            pl.BlockSpec((1, d), lambda i, j: (0, 0)),
            pl.BlockSpec((1, d), lambda i, j: (0, 0)),
            pl.BlockSpec((d, tf), lambda i, j: (0, j)),
            pl.BlockSpec((d, tf), lambda i, j: (0, j)),
            pl.BlockSpec((tf, d), lambda i, j: (j, 0)),
        ],
        out_specs=pl.BlockSpec((tm, d), lambda i, j: (i, 0)),
        scratch_shapes=[pltpu.VMEM((tm, d), MXU_DTYPE)],
        compiler_params=_cparams(("parallel", "arbitrary")),
        name="ffn_half",
    )(x, g_pre.reshape(1, d), g_post.reshape(1, d), w_gate, w_up, w_out_p)


def _norm_mm_kernel(x_ref, g_ref, w_ref, *refs, starts):
    o_refs, h_sc = refs[:-1], refs[-1]
    j = pl.program_id(1)

    @pl.when(j == 0)
    def _():
        h_sc[...] = _rms(x_ref[...], g_ref[...]).astype(h_sc.dtype)

    y = jnp.dot(h_sc[...], w_ref[...], preferred_element_type=F32)
    for o_ref, lo, hi in zip(o_refs, starts[:-1], starts[1:]):
        @pl.when((j >= lo) & (j < hi))
        def _(o_ref=o_ref):
            o_ref[...] = y


def _norm_matmul(x, g, w, tm, splits, tn=1024):
    n, d = x.shape
    p = w.shape[1]
    starts = tuple(int(v) for v in np.cumsum((0,) + tuple(splits)))
    assert starts[-1] * tn == p

    def out_spec(lo, hi):
        return pl.BlockSpec((tm, tn), lambda i, j: (i, jnp.clip(j - lo, 0, hi - lo - 1)))

    return pl.pallas_call(
        functools.partial(_norm_mm_kernel, starts=starts),
        out_shape=tuple(jax.ShapeDtypeStruct((n, c * tn), F32) for c in splits),
        grid=(n // tm, p // tn),
        in_specs=[
            pl.BlockSpec((tm, d), lambda i, j: (i, 0)),
            pl.BlockSpec((1, d), lambda i, j: (0, 0)),
            pl.BlockSpec((d, tn), lambda i, j: (0, j)),
        ],
        out_specs=tuple(out_spec(lo, hi) for lo, hi in zip(starts[:-1], starts[1:])),
        scratch_shapes=[pltpu.VMEM((tm, d), MXU_DTYPE)],
        compiler_params=_cparams(("parallel", "arbitrary")),
        name="norm_matmul",
    )(x, g.reshape(1, d), w)


def _mm_norm_res_kernel(x_ref, o1_ref, o2_ref, w1_ref, w2_ref, g_ref, out_ref):
    y = jnp.dot(o1_ref[...].astype(MXU_DTYPE), w1_ref[...], preferred_element_type=F32)
    y = y + jnp.dot(o2_ref[...].astype(MXU_DTYPE), w2_ref[...], preferred_element_type=F32)
    out_ref[...] = x_ref[...] + _rms(y, g_ref[...])


def _out_proj(x, o1, o2, o2_col, w, g, tm):
    n, d = x.shape
    half = d // 2
    return pl.pallas_call(
        _mm_norm_res_kernel,
        out_shape=jax.ShapeDtypeStruct((n, d), F32),
        grid=(n // tm,),
        in_specs=[
            pl.BlockSpec((tm, d), lambda i: (i, 0)),
            pl.BlockSpec((tm, half), lambda i: (i, 0)),
            pl.BlockSpec((tm, half), lambda i: (i, o2_col)),
            pl.BlockSpec((half, d), lambda i: (0, 0)),
            pl.BlockSpec((half, d), lambda i: (1, 0)),
            pl.BlockSpec((1, d), lambda i: (0, 0)),
        ],
        out_specs=pl.BlockSpec((tm, d), lambda i: (i, 0)),
        compiler_params=_cparams(("parallel",)),
        name="out_proj",
    )(x, o1, o2, w, w, g.reshape(1, d))


def _bucket_vec(dist):
    exact = N_BUCKETS // 2
    d = jnp.maximum(dist, 1).astype(F32)
    far = exact + (jnp.log(d / exact) / math.log(T5_MAX_DIST / exact) * (N_BUCKETS - exact)).astype(jnp.int32)
    return jnp.where(dist < exact, dist, jnp.minimum(far, N_BUCKETS - 1))


def _toeplitz_kernel(par_ref, t5_ref, o_ref, *, rep, head0):
    kind = pl.program_id(1)
    h = head0 + pl.program_id(0) * rep + pl.program_id(2)
    off, max_back, scale, const = par_ref[kind, 0], par_ref[kind, 1], par_ref[kind, 2], par_ref[kind, 3]
    shape = o_ref.shape[2:]
    dist = off + lax.broadcasted_iota(jnp.int32, shape, 0) - lax.broadcasted_iota(jnp.int32, shape, 1)
    valid = ((dist >= 0) & (dist <= max_back)) | (const > 0)
    bkt = _bucket_vec(jnp.where(const > 0, T5_MAX_DIST, jnp.maximum(dist, 0) * scale))
    acc = jnp.zeros(shape, F32)
    for k in range(N_BUCKETS):
        acc = jnp.where(bkt == k, t5_ref[k, h], acc)
    o_ref[0, 0] = jnp.where(valid, acc, NEG)


def _toeplitz_tables(t5_bias, head0, n_groups, rep, rows, cols, kinds):
    par = jnp.asarray(np.asarray(kinds, np.int32))
    smem = pl.BlockSpec(memory_space=pltpu.SMEM)
    return pl.pallas_call(
        functools.partial(_toeplitz_kernel, rep=rep, head0=head0),
        out_shape=jax.ShapeDtypeStruct((n_groups, len(kinds), rep * rows, cols), F32),
        grid=(n_groups, len(kinds), rep),
        in_specs=[smem, smem],
        out_specs=pl.BlockSpec((1, 1, rows, cols), lambda g, k, r: (g, k, r, 0)),
        compiler_params=_cparams(("parallel", "parallel", "parallel")),
        name="bias_toeplitz",
    )(par, t5_bias)


def _row_table_kernel(t5p_ref, o_ref, *, base, step, shift, pick):
    shape = o_ref.shape
    row = lax.broadcasted_iota(jnp.int32, shape, 0)
    bkt = _bucket_vec(base - step * (row >> shift))
    acc = jnp.zeros(shape, F32)
    for k in range(N_BUCKETS):
        acc = jnp.where(bkt == k, t5p_ref[k:k + 1, :], acc)
    if pick:
        lane = lax.broadcasted_iota(jnp.int32, shape, 1)
        one = jnp.sum(jnp.where(lane == (row & ((1 << shift) - 1)), acc, 0.0), axis=1, keepdims=True)
        acc = jnp.broadcast_to(one, shape)
    o_ref[...] = acc


def _row_table(t5p, n, base, step, shift=0, pick=False):
    return pl.pallas_call(
        functools.partial(_row_table_kernel, base=base, step=step, shift=shift, pick=pick),
        out_shape=jax.ShapeDtypeStruct((n, LANES), F32),
        name="bias_rows",
    )(t5p)


def _stack_heads(q, n):
    return jnp.concatenate([q[:, r * HEAD_DIM:(r + 1) * HEAD_DIM] for r in range(n)], axis=0)


def _unstack_heads(o, n):
    t = o.shape[0] // n
    return jnp.concatenate([o[r * t:(r + 1) * t] for r in range(n)], axis=1)


def _rank_lt(score, idx, axis, n_items, k):
    sel = jnp.zeros(score.shape, F32)
    for n in range(n_items):
        sn = score[:, n:n + 1] if axis == 1 else score[n:n + 1, :]
        beats = (score > sn) | ((score == sn) & (idx < n))
        rank = jnp.sum(beats.astype(F32), axis=axis, keepdims=True)
        sel = jnp.where((idx == n) & (rank < k), 1.0, sel)
    return sel


def _rows_to_cols(x_t):
    n, t = x_t.shape
    return jnp.concatenate([x_t, jnp.zeros((LANES - n, t), x_t.dtype)], axis=0).T


def _two_pass_attention(n_tiles, logits_fn, v_fn, s_sc, mx_sc, ls_sc, acc_sc):
    n_pairs = (n_tiles + 1) // 2
    last = n_tiles - 1

    def pair_logits(tp):
        t0, t1 = 2 * tp, 2 * tp + 1
        s0 = logits_fn(t0)
        s1 = jnp.where(t1 <= last, logits_fn(jnp.minimum(t1, last)), NEG)
        s_sc[t0] = s0
        s_sc[t1] = s1
        return jnp.maximum(s0, s1)

    def pass1(tp, c):
        mx_sc[...] = jnp.maximum(mx_sc[...], pair_logits(tp))
        return c

    mx_sc[...] = pair_logits(0)
    lax.fori_loop(1, n_pairs, pass1, 0)
    m = jnp.broadcast_to(jnp.max(mx_sc[...], axis=-1, keepdims=True), mx_sc.shape)

    def pair_pv(tp):
        t0, t1 = 2 * tp, 2 * tp + 1
        p0 = jnp.exp(s_sc[t0] - m)
        p1 = jnp.exp(s_sc[t1] - m)
        pv = (jnp.dot(p0.astype(MXU_DTYPE), v_fn(t0), preferred_element_type=F32)
              + jnp.dot(p1.astype(MXU_DTYPE), v_fn(jnp.minimum(t1, last)), preferred_element_type=F32))
        return p0 + p1, pv

    def pass2(tp, c):
        ps, pv = pair_pv(tp)
        ls_sc[...] += ps
        acc_sc[...] += pv
        return c

    ls_sc[...], acc_sc[...] = pair_pv(0)
    lax.fori_loop(1, n_pairs, pass2, 0)
    return acc_sc[...] / jnp.sum(ls_sc[...], axis=-1, keepdims=True)


def _moba_prompt_kernel(q_ref, k_ref, v_ref, tab_ref, o_ref, kmean_sc, s_sc, mx_sc, ls_sc, acc_sc, *, nblk):
    qi = pl.program_id(2)

    @pl.when(qi == 0)
    def _():
        kmean_sc[...] = jnp.zeros_like(kmean_sc)
        for n in range(nblk):
            blk = k_ref[0, n * MOBA_BLOCK:(n + 1) * MOBA_BLOCK, :]
            kmean_sc[n:n + 1, :] = jnp.sum(blk, axis=0, keepdims=True) * (1.0 / MOBA_BLOCK)

    qs = _stack_heads(q_ref[0], REP_A)
    nb8 = -(-nblk // SUBLANES) * SUBLANES
    gate_t = lax.dot_general(kmean_sc[...], qs, NT, precision=HI, preferred_element_type=F32)[0:nb8]
    blk_id = lax.broadcasted_iota(jnp.int32, gate_t.shape, 0)
    gate_t = jnp.where(blk_id < qi, gate_t, -jnp.inf)
    sel_t = _rank_lt(gate_t, blk_id, 0, nblk, MOBA_TOPK)
    sel_t = jnp.where(blk_id < qi, sel_t, 0.0)
    sel_t = jnp.where(blk_id == qi, 1.0, sel_t)
    sel = _rows_to_cols(sel_t).astype(MXU_DTYPE)

    qb = qs.astype(MXU_DTYPE)
    pick_row = lax.broadcasted_iota(jnp.int32, (LANES, MOBA_BLOCK), 0)

    def logits(j):
        start = pl.multiple_of(j * MOBA_BLOCK, MOBA_BLOCK)
        k = k_ref[0, pl.ds(start, MOBA_BLOCK), :].astype(MXU_DTYPE)
        s = lax.dot_general(qb, k, NT, preferred_element_type=F32) * SCALE + tab_ref[0, jnp.minimum(qi - j, 2)]
        keep = jnp.dot(sel, (pick_row == j).astype(MXU_DTYPE), preferred_element_type=F32)
        return jnp.where(keep > 0.5, s, NEG)

    def values(j):
        start = pl.multiple_of(j * MOBA_BLOCK, MOBA_BLOCK)
        return v_ref[0, pl.ds(start, MOBA_BLOCK), :].astype(MXU_DTYPE)

    out = _two_pass_attention(qi + 1, logits, values, s_sc, mx_sc, ls_sc, acc_sc)
    o_ref[0] = _unstack_heads(out, REP_A)


def _moba_prompt(q3, kv3, tab):
    b, s, _ = q3.shape
    nq = s // TQ
    rows = REP_A * TQ
    return pl.pallas_call(
        functools.partial(_moba_prompt_kernel, nblk=s // MOBA_BLOCK),
        out_shape=jax.ShapeDtypeStruct((b, s, H_A * HEAD_DIM), F32),
        grid=(b, H_KV_A, nq),
        in_specs=[
            pl.BlockSpec((1, TQ, REP_A * HEAD_DIM), lambda bi, h, qi: (bi, qi, h)),
            pl.BlockSpec((1, s, HEAD_DIM), lambda bi, h, qi: (bi, 0, h)),
            pl.BlockSpec((1, s, HEAD_DIM), lambda bi, h, qi: (bi, 0, H_KV_A + h)),
            pl.BlockSpec((1, 3, rows, TQ), lambda bi, h, qi: (h, 0, 0, 0)),
        ],
        out_specs=pl.BlockSpec((1, TQ, REP_A * HEAD_DIM), lambda bi, h, qi: (bi, qi, h)),
        scratch_shapes=[pltpu.VMEM((LANES, HEAD_DIM), F32), pltpu.VMEM((nq, rows, TQ), F32),
                        pltpu.VMEM((rows, TQ), F32), pltpu.VMEM((rows, TQ), F32),
                        pltpu.VMEM((rows, HEAD_DIM), F32)],
        compiler_params=_cparams(("parallel", "parallel", "arbitrary")),
        name="moba_prompt",
    )(q3, kv3, kv3, tab)


def _compress_seq(x_refs, w1_ref, w2_ref, pe_ref, slot):
    n_rows = 128
    ng = len(x_refs)
    acc_a = jnp.zeros((ng * n_rows, HEAD_DIM), F32)
    acc_b = jnp.zeros((ng * n_rows, HEAD_DIM), F32)
    for u in range(CMP_STRIDE):
        xs = jnp.concatenate([x[pl.ds(u, n_rows, stride=CMP_STRIDE), :] for x in x_refs], axis=0)
        xs = xs.astype(MXU_DTYPE)
        acc_a += jnp.dot(xs, w1_ref[slot, u * HEAD_DIM:(u + 1) * HEAD_DIM, :], preferred_element_type=F32)
        acc_b += jnp.dot(xs, w1_ref[slot, (CMP_STRIDE + u) * HEAD_DIM:(CMP_STRIDE + u + 1) * HEAD_DIM, :],
                         preferred_element_type=F32)
    pe8 = jnp.broadcast_to(pe_ref[slot], (SUBLANES, CMP_LEN * HEAD_DIM)).astype(MXU_DTYPE)
    c = jnp.dot(pe8, w1_ref[slot], preferred_element_type=F32)[0:1]
    outs = []
    for g in range(ng):
        a = acc_a[g * n_rows:(g + 1) * n_rows]
        bsh = pltpu.roll(acc_b[g * n_rows:(g + 1) * n_rows], n_rows - 1, 0)
        y = a + bsh + c
        hid = (y * jax.nn.sigmoid(y)).astype(MXU_DTYPE)
        outs.append(jnp.dot(hid, w2_ref[slot], preferred_element_type=F32))
    return jnp.concatenate(outs, axis=1) if ng > 1 else outs[0]


def _cover_matrix(n_cmp, n_slc, transposed):
    shape = (LANES, LANES)
    n = lax.broadcasted_iota(jnp.int32, shape, 1 if transposed else 0)
    s = lax.broadcasted_iota(jnp.int32, shape, 0 if transposed else 1)
    c = ((n * CMP_STRIDE < (s + 1) * SLC_BLOCK) & (n * CMP_STRIDE + CMP_LEN > s * SLC_BLOCK)
         & (n < n_cmp) & (s < n_slc))
    return c.astype(F32)


def _slc_scores(imp_t, pos, n_slc):
    blk = lax.broadcasted_iota(jnp.int32, imp_t.shape, 0)
    own = pos >> _log2(SLC_BLOCK)
    forced = (blk == 0) | (blk == own) | (blk == own - 1)
    score = jnp.where(forced, 1e30, jnp.where(blk > own, -1e30, imp_t))
    return jnp.where(blk < n_slc, score, -2e30), blk


def _nsa_prompt_kernel(q_ref, kc_ref, vc_ref, ks_ref, vs_ref, kw_ref, vw_ref, gate_ref, pe_ref, w1_ref,
                       w2_ref, tabs_ref, tabw_ref, o_ref, kcmp_sc, vcmp_sc, s_sc, mx_sc, ls_sc, acc_sc,
                       *, n_cmp, n_slc):
    g = pl.program_id(1)
    qi = pl.program_id(2)

    @pl.when(qi == 0)
    def _():
        kcmp_sc[...] = _compress_seq([kc_ref.at[0]], w1_ref, w2_ref, pe_ref, 0)
        vcmp_sc[...] = _compress_seq([vc_ref.at[0]], w1_ref, w2_ref, pe_ref, 1)

    rows = REP_B * TQ
    qs = _stack_heads(q_ref[0], REP_B)
    qb = qs.astype(MXU_DTYPE)

    s = lax.dot_general(qb, kcmp_sc[...].astype(MXU_DTYPE), NT, preferred_element_type=F32) * SCALE
    t_rows = qi * TQ + (lax.broadcasted_iota(jnp.int32, (rows, LANES), 0) & (TQ - 1))
    lane_rows = lax.broadcasted_iota(jnp.int32, (rows, LANES), 1)
    ok = (lane_rows * CMP_STRIDE + CMP_LEN - 1 <= t_rows) & (lane_rows < n_cmp)
    s = jnp.where(ok, s, NEG)
    m = jnp.max(s, axis=-1, keepdims=True)
    e = jnp.where(ok, jnp.exp(s - m), 0.0)
    p = e / jnp.maximum(jnp.sum(e, axis=-1, keepdims=True), TINY)
    o_cmp = jnp.dot(p.astype(MXU_DTYPE), vcmp_sc[...].astype(MXU_DTYPE), preferred_element_type=F32)

    psum = p[0:TQ]
    for r in range(1, REP_B):
        psum = psum + p[r * TQ:(r + 1) * TQ]
    imp_t = lax.dot_general(_cover_matrix(n_cmp, n_slc, True), psum, NT, precision=HI,
                            preferred_element_type=F32)
    ns8 = -(-n_slc // SUBLANES) * SUBLANES
    pos = qi * TQ + lax.broadcasted_iota(jnp.int32, (ns8, TQ), 1)
    score, blk = _slc_scores(imp_t[0:ns8], pos, n_slc)
    sel = _rows_to_cols(_rank_lt(score, blk, 0, n_slc, SLC_TOPK)).astype(MXU_DTYPE)

    blk_of_lane = lax.broadcasted_iota(jnp.int32, (LANES, TQ), 1) >> _log2(SLC_BLOCK)
    blk_row = lax.broadcasted_iota(jnp.int32, (LANES, TQ), 0)

    def sel_logits(j):
        start = pl.multiple_of(j * TQ, TQ)
        k = ks_ref[0, pl.ds(start, TQ), :].astype(MXU_DTYPE)
        expand = (blk_row == blk_of_lane + j * (TQ // SLC_BLOCK)).astype(MXU_DTYPE)
        keep1 = jnp.dot(sel, expand, preferred_element_type=F32)
        keep = jnp.concatenate([keep1] * REP_B, axis=0) > 0.5
        sc = lax.dot_general(qb, k, NT, preferred_element_type=F32) * SCALE + tabs_ref[0, jnp.minimum(qi - j, 2)]
        return jnp.where(keep, sc, NEG)

    def sel_values(j):
        return vs_ref[0, pl.ds(pl.multiple_of(j * TQ, TQ), TQ), :].astype(MXU_DTYPE)

    o_sel = _two_pass_attention(qi + 1, sel_logits, sel_values, s_sc, mx_sc, ls_sc, acc_sc)

    def win_logits(d):
        start = pl.multiple_of((qi - d) * TQ, TQ)
        k = kw_ref[0, pl.ds(start, TQ), :].astype(MXU_DTYPE)
        return lax.dot_general(qb, k, NT, preferred_element_type=F32) * SCALE + tabw_ref[0, d]

    def win_values(d):
        return vw_ref[0, pl.ds(pl.multiple_of((qi - d) * TQ, TQ), TQ), :].astype(MXU_DTYPE)

    o_win = _two_pass_attention(jnp.minimum(qi, WIN_B // TQ) + 1, win_logits, win_values,
                                s_sc, mx_sc, ls_sc, acc_sc)

    sg = jax.nn.sigmoid(gate_ref[0])
    lane = lax.broadcasted_iota(jnp.int32, (TQ, LANES), 1)
    outs = []
    for r in range(REP_B):
        base = (g * REP_B + r) * 3
        mix = jnp.zeros((TQ, HEAD_DIM), F32)
        for c, o in enumerate((o_cmp, o_sel, o_win)):
            gc = jnp.sum(jnp.where(lane == base + c, sg, 0.0), axis=1, keepdims=True)
            mix = mix + gc * o[r * TQ:(r + 1) * TQ]
        outs.append(mix)
    o_ref[0] = jnp.concatenate(outs, axis=1)


def _nsa_prompt(q3, kv3, wg3, pe_flat, w1, w2, tab_sel, tab_win):
    b, s, _ = q3.shape
    nq = s // TQ
    rows = REP_B * TQ
    n_cmp = (s - CMP_LEN) // CMP_STRIDE + 1
    n_slc = -(-s // SLC_BLOCK)

    def seq_spec(slot):
        return pl.BlockSpec((1, s, HEAD_DIM), lambda bi, g, qi: (bi, 0, slot * G_B + g))

    def full(a):
        return pl.BlockSpec(a.shape, lambda bi, g, qi: (0,) * a.ndim)

    qw = REP_B * HEAD_DIM
    return pl.pallas_call(
        functools.partial(_nsa_prompt_kernel, n_cmp=n_cmp, n_slc=n_slc),
        out_shape=jax.ShapeDtypeStruct((b, s, H_B * HEAD_DIM), F32),
        grid=(b, G_B, nq),
        in_specs=[
            pl.BlockSpec((1, TQ, qw), lambda bi, g, qi: (bi, qi, g)),
            seq_spec(0), seq_spec(1), seq_spec(2), seq_spec(3), seq_spec(0), seq_spec(1),
            pl.BlockSpec((1, TQ, LANES), lambda bi, g, qi: (bi, qi, 2 * G_B)),
            full(pe_flat), full(w1), full(w2),
            pl.BlockSpec((1, 3, rows, TQ), lambda bi, g, qi: (g, 0, 0, 0)),
            pl.BlockSpec((1, 3, rows, TQ), lambda bi, g, qi: (g, 0, 0, 0)),
        ],
        out_specs=pl.BlockSpec((1, TQ, qw), lambda bi, g, qi: (bi, qi, g)),
        scratch_shapes=[pltpu.VMEM((LANES, HEAD_DIM), F32), pltpu.VMEM((LANES, HEAD_DIM), F32),
                        pltpu.VMEM((nq, rows, TQ), F32), pltpu.VMEM((rows, TQ), F32),
                        pltpu.VMEM((rows, TQ), F32), pltpu.VMEM((rows, HEAD_DIM), F32)],
        compiler_params=_cparams(("parallel", "parallel", "arbitrary")),
        name="nsa_prompt",
    )(q3, kv3, kv3, kv3, kv3, wg3, wg3, wg3, pe_flat, w1, w2, tab_sel, tab_win)


def _rows(start, size, stride):
    return pl.ds(start, size) if stride == 1 else pl.ds(start, size, stride=stride)


def _dil_fused_kernel(q_ref, k_ref, v_ref, tab_ref, o_ref, m_sc, l_sc):
    s_len = q_ref.shape[1]
    for pat, (win, dil) in enumerate(C_PATTERNS):
        sf = s_len // dil
        for r in range(dil):
            for t in range(sf // BAND):
                rows = _rows(r + dil * BAND * t, BAND, dil)
                q = q_ref[0, rows, :].astype(MXU_DTYPE)
                if t == 0:
                    keys = rows
                    bias = tab_ref[0, pat, :, BAND:]
                else:
                    keys = _rows(r + dil * BAND * (t - 1), 2 * BAND, dil)
                    bias = tab_ref[0, pat]
                k = k_ref[0, keys, :].astype(MXU_DTYPE)
                v = v_ref[0, keys, :].astype(MXU_DTYPE)
                s = lax.dot_general(q, k, NT, preferred_element_type=F32) * SCALE + bias
                m_t = jnp.broadcast_to(jnp.max(s, axis=-1, keepdims=True), (BAND, LANES))
                if pat == 0:
                    m_new = m_t
                else:
                    m_old = m_sc[rows, :]
                    m_new = jnp.maximum(m_old, m_t)
                    alpha = jnp.exp(m_old - m_new)
                e = jnp.exp(s - jnp.concatenate([m_new] * (s.shape[1] // LANES), axis=1))
                l_t = jnp.broadcast_to(jnp.sum(e, axis=-1, keepdims=True), (BAND, LANES))
                pv = jnp.dot(e.astype(MXU_DTYPE), v, preferred_element_type=F32)
                if pat == 0:
                    l_sc[rows, :] = l_t
                    o_ref[0, rows, :] = pv
                else:
                    l_sc[rows, :] = alpha * l_sc[rows, :] + l_t
                    o_ref[0, rows, :] = alpha * o_ref[0, rows, :] + pv
                m_sc[rows, :] = m_new
    o_ref[0] = o_ref[0] / l_sc[...]


def _dil_fused(q3, kv3, tab):
    b, s, hw = q3.shape
    seq = pl.BlockSpec((1, s, HEAD_DIM), lambda bi, h: (bi, 0, h))
    return pl.pallas_call(
        _dil_fused_kernel,
        out_shape=jax.ShapeDtypeStruct((b, s, hw), F32),
        grid=(b, H_C),
        in_specs=[seq, seq, pl.BlockSpec((1, s, HEAD_DIM), lambda bi, h: (bi, 0, H_C + h)),
                  pl.BlockSpec((1,) + tab.shape[1:], lambda bi, h: (h, 0, 0, 0))],
        out_specs=seq,
        scratch_shapes=[pltpu.VMEM((s, LANES), F32), pltpu.VMEM((s, LANES), F32)],
        compiler_params=_cparams(("parallel", "parallel")),
        name="dilated_prompt",
    )(q3, kv3, kv3, tab)


QROWS = 16


def _ab_sample(zs, cache_a, cache_b, win_buf, layer, page_table, pe_flat, w1, w2, tabs):
    db = zs[0].shape[0]
    assert sum(z.shape[1] for z in zs) == P_AB_PAD
    n_pages = page_table.shape[1]
    n_even, n_pool = cache_a.shape[:2]
    rows_a = PAGE * 2 * H_KV_A
    rows_b = PAGE * 4 * G_B
    ca = cache_a.reshape(n_even, n_pool * rows_a, HEAD_DIM)
    cb = cache_b.reshape(n_even, n_pool * rows_b, HEAD_DIM)
    rows_w = win_buf.shape[2] * 2 * G_B
    win = win_buf.reshape(n_even, db, rows_w, HEAD_DIM)
    width = (H_A + H_B) * HEAD_DIM

    def page_spec(rows, j):
        return pl.BlockSpec((None, rows, HEAD_DIM), lambda b, pt: (layer, pt[b, j], 0))

    def full(a):
        return pl.BlockSpec(a.shape, lambda b, pt: (0,) * a.ndim)

    grid_spec = pltpu.PrefetchScalarGridSpec(
        num_scalar_prefetch=1,
        grid=(db,),
        in_specs=([full(z) for z in zs]
                  + [page_spec(rows_a, j) for j in range(n_pages)]
                  + [page_spec(rows_b, j) for j in range(n_pages)]
                  + [pl.BlockSpec((None, None, rows_w, HEAD_DIM), lambda b, pt: (layer, b, 0, 0)),
                     full(pe_flat), full(w1), full(w2)] + [full(t) for t in tabs]),
        out_specs=pl.BlockSpec((db, width), lambda b, pt: (0, 0)),
        scratch_shapes=[pltpu.VMEM((2 * G_B, n_pages * PAGE, HEAD_DIM), F32)],
    )
    return pl.pallas_call(
        functools.partial(_ab_sample_kernel, n_pages=n_pages, n_z=len(zs)),
        out_shape=jax.ShapeDtypeStruct((db, width), F32),
        grid_spec=grid_spec,
        compiler_params=_cparams(("arbitrary",)),
        name="ab_sample",
    )(page_table, *zs, *([ca] * n_pages), *([cb] * n_pages), win, pe_flat, w1, w2, *tabs)


def _rows_up(x, shift):
    x3 = x.reshape(x.shape[0] // SUBLANES, SUBLANES, x.shape[1])
    return pltpu.roll(x3, SUBLANES - shift, 1).reshape(x.shape)


def _stored_attention(q, blocks, k_to_v, tab, sel, blk_cols, lself, v_new, chunk=4):
    qb = q.astype(MXU_DTYPE)
    s = jnp.concatenate([lax.dot_general(qb, blk[...].astype(MXU_DTYPE), NT, preferred_element_type=F32)
                         for blk in blocks], axis=1) * SCALE + tab
    if sel is not None:
        n_cols = s.shape[1]
        s = jnp.concatenate([jnp.where(sel[:, n:n + 1] > 0.5, s[:, n * blk_cols:(n + 1) * blk_cols], NEG)
                             for n in range(n_cols // blk_cols)], axis=1)
    m = jnp.maximum(jnp.max(s, axis=1, keepdims=True), lself)
    e = jnp.exp(s - m).astype(MXU_DTYPE)
    es = jnp.exp(lself - m)
    den = jnp.sum(e.astype(F32), axis=1, keepdims=True) + es
    acc = es * v_new
    col = 0
    for i in range(0, len(blocks), chunk):
        vals = jnp.concatenate([_rows_up(blk[...], k_to_v).astype(MXU_DTYPE)
                                for blk in blocks[i:i + chunk]], axis=0)
        acc += jnp.dot(e[:, col:col + vals.shape[0]], vals, preferred_element_type=F32)
        col += vals.shape[0]
    return acc / den


def _rows_by(z, col0, idx_of_row, n):
    rows = [z[:, col0 + idx_of_row(r) * HEAD_DIM:col0 + (idx_of_row(r) + 1) * HEAD_DIM] for r in range(n)]
    return jnp.concatenate(rows + [jnp.zeros((QROWS - n, HEAD_DIM), F32)], axis=0)


def _ab_sample_kernel(pt_ref, *refs, n_pages, n_z):
    z_refs, refs = refs[:n_z], refs[n_z:]
    ca = refs[:n_pages]
    cb = refs[n_pages:2 * n_pages]
    (win_ref, pe_ref, w1_ref, w2_ref, taba_ref, tabs_ref, tabw_ref, selfa_ref, selfb_ref, o_ref, seq_sc) = \
        refs[2 * n_pages:]
    del pt_ref
    b = pl.program_id(0)
    z = jnp.concatenate([r[pl.ds(b, 1), :] for r in z_refs], axis=1)
    past = n_pages * PAGE
    row_id = lax.broadcasted_iota(jnp.int32, (QROWS, LANES), 0)
    lane_id = lax.broadcasted_iota(jnp.int32, (QROWS, LANES), 1)

    rpp_a = 2 * H_KV_A
    assert rpp_a == SUBLANES and 4 * G_B == SUBLANES
    pages_per_blk = MOBA_BLOCK // PAGE
    nblk = past // MOBA_BLOCK
    q = _rows_by(z, A_Q, lambda r: r, H_A)
    ksum = [jnp.sum(pg[...].reshape(PAGE, rpp_a, HEAD_DIM), axis=0) for pg in ca]
    kmean = [sum(ksum[n * pages_per_blk + i] for i in range(pages_per_blk)) * (1.0 / MOBA_BLOCK)
             for n in range(nblk)]
    kmean = jnp.concatenate(kmean + [jnp.zeros((LANES - nblk * rpp_a, HEAD_DIM), F32)], axis=0)
    g_all = lax.dot_general(q, kmean, NT, precision=HI, preferred_element_type=F32)
    gate = jnp.full((QROWS, LANES), -jnp.inf, F32)
    for n in range(nblk):
        pick = lane_id == n * rpp_a + (row_id >> _log2(REP_A))
        gate = jnp.where(lane_id == n, jnp.sum(jnp.where(pick, g_all, 0.0), axis=1, keepdims=True), gate)
    sel = _rank_lt(gate, lane_id, 1, nblk, MOBA_TOPK)
    lself = (jnp.sum(q * _rows_by(z, A_K, lambda r: r // REP_A, H_A), axis=1, keepdims=True) * SCALE
             + selfa_ref[:, 0:1])
    o_a = _stored_attention(q, ca, H_KV_A, taba_ref[...], sel, MOBA_BLOCK * rpp_a, lself,
                            _rows_by(z, A_V, lambda r: r // REP_A, H_A))

    rpp_b = 4 * G_B
    n_cmp = (past + 1 - CMP_LEN) // CMP_STRIDE + 1
    n_slc = -(-(past + 1) // SLC_BLOCK)
    own = past // SLC_BLOCK
    for p in range(n_pages):
        for c in range(2 * G_B):
            seq_sc[c, p * PAGE:(p + 1) * PAGE, :] = cb[p][pl.ds(c, PAGE, stride=rpp_b), :]
    kcmp = _compress_seq([seq_sc.at[g] for g in range(G_B)], w1_ref, w2_ref, pe_ref, 0)
    vcmp = _compress_seq([seq_sc.at[G_B + g] for g in range(G_B)], w1_ref, w2_ref, pe_ref, 1)
    q = _rows_by(z, B_Q, lambda r: r, H_B)
    qb = q.astype(MXU_DTYPE)
    grp = row_id >> _log2(REP_B)

    def by_group(fn):
        out = fn(0)
        for g in range(1, G_B):
            out = jnp.where(grp == g, fn(g), out)
        return out

    sc = by_group(lambda g: lax.dot_general(qb, kcmp[:, g * HEAD_DIM:(g + 1) * HEAD_DIM].astype(MXU_DTYPE), NT,
                                            preferred_element_type=F32)) * SCALE
    okc = (lane_id < n_cmp) & (lane_id * CMP_STRIDE + CMP_LEN - 1 <= past)
    sc = jnp.where(okc, sc, NEG)
    ec = jnp.where(okc, jnp.exp(sc - jnp.max(sc, axis=1, keepdims=True)), 0.0)
    pc = ec / jnp.maximum(jnp.sum(ec, axis=1, keepdims=True), TINY)
    pcb = pc.astype(MXU_DTYPE)
    o_cmp = by_group(lambda g: jnp.dot(pcb, vcmp[:, g * HEAD_DIM:(g + 1) * HEAD_DIM].astype(MXU_DTYPE),
                                       preferred_element_type=F32))
    psum = by_group(lambda g: jnp.sum(jnp.where((grp == g) & (row_id < H_B), pc, 0.0), axis=0, keepdims=True))
    imp = jnp.dot(psum, _cover_matrix(n_cmp, n_slc, False), precision=HI, preferred_element_type=F32)
    forced = (lane_id == 0) | (lane_id == own) | (lane_id == own - 1)
    score = jnp.where(forced, 1e30, jnp.where(lane_id > own, -1e30, imp))
    score = jnp.where(lane_id < n_slc, score, -2e30)
    sel = _rank_lt(score, lane_id, 1, n_slc, SLC_TOPK)
    lself = (jnp.sum(q * _rows_by(z, B_KS, lambda r: r // REP_B, H_B), axis=1, keepdims=True) * SCALE
             + selfb_ref[:, 0:1])
    o_sel = _stored_attention(q, cb, G_B, tabs_ref[...], sel, SLC_BLOCK * rpp_b, lself,
                              _rows_by(z, B_VS, lambda r: r // REP_B, H_B))
    lself = (jnp.sum(q * _rows_by(z, B_KW, lambda r: r // REP_B, H_B), axis=1, keepdims=True) * SCALE
             + selfb_ref[:, 0:1])
    o_win = _stored_attention(q, [win_ref], G_B, tabw_ref[...], None, 0, lself,
                              _rows_by(z, B_VW, lambda r: r // REP_B, H_B), chunk=1)
    sg = jax.nn.sigmoid(z[:, B_GATE:B_GATE + LANES])
    o_b = jnp.zeros((QROWS, HEAD_DIM), F32)
    for c, o in enumerate((o_cmp, o_sel, o_win)):
        o_b = o_b + jnp.sum(jnp.where(lane_id == row_id * 3 + c, sg, 0.0), axis=1, keepdims=True) * o
    o_ref[pl.ds(b, 1), :] = jnp.concatenate([o_a[r:r + 1] for r in range(H_A)] + [o_b[r:r + 1] for r in range(H_B)],
                                            axis=1)


def _sample_tables_ab(t5_bias, past, wbuf):
    def stored(t5_heads, n, rows_per_pos, key_row_of_head):
        nh = t5_heads.shape[1]
        base = _row_table(jnp.pad(t5_heads, ((0, 0), (0, LANES - nh))), n, n, 1)[:, :nh].T
        j = np.arange(rows_per_pos)[None, None, :]
        mine = j == np.asarray([key_row_of_head(h) for h in range(nh)])[:, None, None]
        tab = jnp.where(jnp.asarray(mine), base[:, :, None], NEG).reshape(nh, n * rows_per_pos)
        return jnp.pad(tab, ((0, QROWS - nh), (0, 0)))
    t5a, t5b = t5_bias[:, :H_A], t5_bias[:, H_A:H_A + H_B]
    tab_a = stored(t5a, past, 2 * H_KV_A, lambda h: h // REP_A)
    tab_s = stored(t5b, past, 4 * G_B, lambda h: 2 * G_B + h // REP_B)
    tab_w = stored(t5b, wbuf, 2 * G_B, lambda h: h // REP_B)
    self_a = jnp.broadcast_to(jnp.pad(t5a[0], (0, QROWS - H_A))[:, None], (QROWS, LANES))
    self_b = jnp.broadcast_to(jnp.pad(t5b[0], (0, QROWS - H_B))[:, None], (QROWS, LANES))
    return tab_a, tab_s, tab_w, self_a, self_b


def _dil_sample_kernel(q_ref, kv_ref, b1_ref, b2_ref, b3_ref, tab_ref, self_ref, o_ref):
    nh = H_C
    q = q_ref[...]
    k_new = kv_ref[0:nh, :]
    v_new = kv_ref[nh:2 * nh, :]
    ones = jnp.ones((HEAD_DIM, LANES), MXU_DTYPE)

    def lane_sum(x):
        return jnp.dot(x.astype(MXU_DTYPE), ones, preferred_element_type=F32)

    n_pat = len(C_PATTERNS)
    lself = lane_sum(k_new * q) * SCALE + self_ref[...]
    bufs = (b1_ref, b2_ref, b3_ref)
    lgs = []
    for i, buf in enumerate(bufs):
        prod = buf[:, 0:nh, :] * q[None]
        lg = lane_sum(prod.reshape(BAND * nh, HEAD_DIM)) * SCALE + tab_ref[i]
        lgs.append(lg.reshape(BAND, nh, LANES))
    m = lself
    for x in lgs:
        m = jnp.maximum(m, jnp.max(x, axis=0))
    es = jnp.exp(lself - m) * float(n_pat)
    den = es
    acc = es * v_new
    for x, buf in zip(lgs, bufs):
        ee = jnp.exp(x - m[None])
        den = den + jnp.sum(ee, axis=0)
        acc = acc + jnp.sum(ee * buf[:, nh:2 * nh, :], axis=0)
    o_ref[...] = acc / den


def _dil_sample(qs, kvs, buf_all, layer, tab, tab_self):
    db = qs.shape[0]
    n_odd, _, wc = buf_all.shape[:3]
    rpp = 2 * H_C
    q3 = qs.reshape(db, H_C, HEAD_DIM)
    kv3 = kvs.reshape(db, rpp, HEAD_DIM)
    specs, views = [], []
    for win, dil in C_PATTERNS:
        nrows = wc // dil
        views.append(buf_all.reshape(n_odd, db, nrows, dil, rpp, HEAD_DIM))
        specs.append(pl.BlockSpec((None, None, BAND, None, rpp, HEAD_DIM),
                                  lambda b, blk=nrows // BAND - 1: (layer, b, blk, 0, 0, 0)))
    out = pl.pallas_call(
        _dil_sample_kernel,
        out_shape=jax.ShapeDtypeStruct((db, H_C, HEAD_DIM), F32),
        grid=(db,),
        in_specs=[pl.BlockSpec((None, H_C, HEAD_DIM), lambda b: (b, 0, 0)),
                  pl.BlockSpec((None, rpp, HEAD_DIM), lambda b: (b, 0, 0))] + specs
                 + [pl.BlockSpec(tab.shape, lambda b: (0, 0, 0)), pl.BlockSpec(tab_self.shape, lambda b: (0, 0))],
        out_specs=pl.BlockSpec((None, H_C, HEAD_DIM), lambda b: (b, 0, 0)),
        compiler_params=_cparams(("parallel",)),
        name="dilated_sample",
    )(q3, kv3, *views, tab, tab_self)
    return out.reshape(db, H_C * HEAD_DIM)


def _sample_tables_c(t5_bias):
    t5c = jnp.pad(t5_bias[:, :H_C], ((0, 0), (0, LANES - H_C)))
    shift = _log2(H_C)
    tabs = [_row_table(t5c, BAND * H_C, dil * BAND, dil, shift, True) for _, dil in C_PATTERNS]
    tab_self = jnp.broadcast_to(t5_bias[0, :H_C][:, None], (H_C, LANES))
    return jnp.stack(tabs), tab_self


def _pad_cols(w, to):
    return jnp.pad(w, ((0, 0), (0, to - w.shape[1])))


def kernel(x_prompt, x_sample, cache_a_kv, cache_b_kv, state_b_win, state_c_kv, page_table, t5_bias, norm_g,
           w_ffn_in, w_ffn_out, w_in_ab, w_out_ab, cmp_pe, w_cmp_1, w_cmp_2, w_in_c, w_out_c):
    b, s, d = x_prompt.shape
    db = x_sample.shape[0]
    depth = norm_g.shape[0]
    d_ff = w_ffn_out.shape[2]
    ffp = -(-d_ff // 512) * 512
    past = page_table.shape[1] * cache_a_kv.shape[2]
    assert x_sample.shape[1] == 1 and s % TQ == 0 and past % MOBA_BLOCK == 0
    assert state_c_kv.shape[2] == C_PATTERNS[-1][0] and state_b_win.shape[2] == WIN_B
    assert all(w // dl == BAND for w, dl in C_PATTERNS)

    xp = x_prompt.reshape(b * s, d)
    xs = x_sample.reshape(db, d)
    tm_ffn, tm_p, tm_s = 512, 512, db

    def ffn_weights(layer, half):
        w_in = w_ffn_in[layer, half]
        w_gate = _pad_cols(w_in[:, :d_ff].astype(MXU_DTYPE), ffp)
        w_up = _pad_cols(w_in[:, d_ff:].astype(MXU_DTYPE), ffp)
        w_out_p = jnp.pad(w_ffn_out[layer, half].astype(MXU_DTYPE), ((0, ffp - d_ff), (0, 0)))
        return w_gate, w_up, w_out_p

    outs = {}
    for layer in range(depth):
        i = layer // 2
        g = norm_g[layer]
        w_ffn = ffn_weights(layer, 0)
        xp = _ffn_half(xp, g[0], g[1], w_ffn, tm_ffn)
        xs = _ffn_half(xs, g[0], g[1], w_ffn, tm_s)
        if layer % 2 == 0:
            w_in = _pad_cols(w_in_ab[i], P_AB_PAD).astype(MXU_DTYPE)
            ab_groups = (1,) * (P_AB_PAD // 1024)
            zp = [z.reshape(b, s, 1024) for z in _norm_matmul(xp, g[2], w_in, tm_p, ab_groups)]
            zs = _norm_matmul(xs, g[2], w_in, tm_s, ab_groups)
            near =[(0, NO_LIMIT, 1, 0), (TQ, NO_LIMIT, 1, 0), (0, NO_LIMIT, 1, 1)]
            tab_moba = _toeplitz_tables(t5_bias, 0, H_KV_A, REP_A, TQ, TQ, near)
            tab_sel = _toeplitz_tables(t5_bias, H_A, G_B, REP_B, TQ, TQ, near)
            tab_win = _toeplitz_tables(t5_bias, H_A, G_B, REP_B, TQ, TQ,
                                       [(d_ * TQ, WIN_B, 1, 0) for d_ in range(WIN_B // TQ + 1)])
            pe_flat = cmp_pe[i].reshape(2, 1, CMP_LEN * HEAD_DIM)
            w1 = w_cmp_1[i].astype(MXU_DTYPE)
            w2 = w_cmp_2[i].astype(MXU_DTYPE)
            o_a = _moba_prompt(zp[0], zp[1], tab_moba).reshape(b * s, H_A * HEAD_DIM)
            o_b = _nsa_prompt(zp[2], zp[3], zp[4], pe_flat, w1, w2, tab_sel, tab_win).reshape(b * s, H_B * HEAD_DIM)
            w_out = w_out_ab[i].astype(MXU_DTYPE)
            xp = _out_proj(xp, o_a, o_b, 0, w_out, g[3], tm_p)
            o_s = _ab_sample(zs, cache_a_kv, cache_b_kv, state_b_win, i, page_table, pe_flat, w1, w2,
                             _sample_tables_ab(t5_bias, past, state_b_win.shape[2]))
            xs = _out_proj(xs, o_s, o_s, 1, w_out, g[3], db)
            outs.setdefault("a_p", []).append(zp[1].reshape(b, s, 2, H_KV_A, HEAD_DIM))
            outs.setdefault("a_s", []).append(zs[1].reshape(db, 1, 2, H_KV_A, HEAD_DIM))
            outs.setdefault("b_p", []).append(zp[3].reshape(b, s, 4, G_B, HEAD_DIM))
            outs.setdefault("b_s", []).append(zs[3].reshape(db, 1, 4, G_B, HEAD_DIM))
            nw = min(WIN_B, s)
            ww = B_GATE - B_KW
            outs.setdefault("w_p", []).append(zp[4][:, s - nw:, :ww].reshape(b, nw, 2, G_B, HEAD_DIM))
            outs.setdefault("w_s", []).append(zs[4][:, :ww].reshape(db, 1, 2, G_B, HEAD_DIM))
        else:
            w_in = w_in_c[i].astype(MXU_DTYPE)
            hw = H_C * HEAD_DIM
            c_groups = (hw // 1024, 2 * hw // 1024)
            qp, kvp = _norm_matmul(xp, g[2], w_in, tm_p, c_groups)
            qs, kvs = _norm_matmul(xs, g[2], w_in, tm_s, c_groups)
            kvp3 = kvp.reshape(b, s, 2 * hw)
            tab_dil = _toeplitz_tables(t5_bias, 0, H_C, 1, BAND, 2 * BAND,
                                       [(BAND, BAND, dl, 0) for _, dl in C_PATTERNS])
            o_c = _dil_fused(qp.reshape(b, s, hw), kvp3, tab_dil).reshape(b * s, hw)
            w_out = w_out_c[i].astype(MXU_DTYPE)
            xp = _out_proj(xp, o_c, o_c, 1, w_out, g[3], tm_p)
            tab_c, self_c = _sample_tables_c(t5_bias)
            o_s = _dil_sample(qs, kvs, state_c_kv, i, tab_c, self_c)
            xs = _out_proj(xs, o_s, o_s, 1, w_out, g[3], db)
            nc = min(C_PATTERNS[-1][0], s)
            outs.setdefault("c_p", []).append(kvp3[:, s - nc:].reshape(b, nc, 2, H_C, HEAD_DIM))
            outs.setdefault("c_s", []).append(kvs.reshape(db, 1, 2, H_C, HEAD_DIM))
        w_ffn = ffn_weights(layer, 1)
        xp = _ffn_half(xp, g[4], g[5], w_ffn, tm_ffn)
        xs = _ffn_half(xs, g[4], g[5], w_ffn, tm_s)
    return (xp.reshape(b, s, d), xs.reshape(db, 1, d)) + tuple(
        jnp.stack(outs[k]) for k in ("a_p", "a_s", "b_p", "b_s", "w_p", "w_s", "c_p", "c_s"))
```
